```python
import jax
import jax.numpy as jnp
from jax import lax
import numpy as np

D_MODEL = 1024
BATCH = 16
SEQ = 4096
DEPTH = 1
DEC_BATCH = 128
DEC_SEQ = 8
PAST_LEN = 8192
PAGE_SIZE = 128

HEAD_DIM = 64
RW_HEADS = D_MODEL // 128
RW_WIDTH = RW_HEADS * HEAD_DIM
SB_HEADS = D_MODEL // 128
SB_WIDTH = SB_HEADS * HEAD_DIM
W_RANK = 64
A_RANK = 64
RW_PROJ = 3 * RW_WIDTH + W_RANK + A_RANK
MIX_WIDTH = RW_WIDTH + SB_WIDTH
IN_WIDTH = RW_PROJ + 3 * SB_WIDTH + MIX_WIDTH
IN_SPLITS = (RW_PROJ, RW_PROJ + SB_WIDTH, RW_PROJ + 2 * SB_WIDTH, RW_PROJ + 3 * SB_WIDTH,
             RW_PROJ + 3 * SB_WIDTH + RW_WIDTH)
RW_SPLITS = (RW_WIDTH, 2 * RW_WIDTH, 3 * RW_WIDTH, 3 * RW_WIDTH + W_RANK)
DECAY_OFFSET = 0.5
GN_EPS = 64e-5
RMS_EPS = 1e-6
SB_SCALE = HEAD_DIM ** -0.5
SB_BIAS_INIT = -9.0
Q_BLOCK = 128
N_EXPERTS = 32
TOP_K = 4
D_FF = D_MODEL
SWIGLU_ALPHA = 1.702
SWIGLU_LIMIT = 7.0
MOE_BLOCK = 128

kernel_name = 'rwkv7_stickbreak_moe_hybrid_step'


def _rmsnorm(x, g, eps=RMS_EPS):
    xf = x.astype(jnp.float32)
    y = xf * lax.rsqrt(jnp.mean(xf * xf, axis=-1, keepdims=True) + eps)
    return (y * g.astype(jnp.float32)).astype(x.dtype)


def _wkv7_scan(S0, r, w, k, v, kk, a):
    xs = tuple(jnp.moveaxis(t, 1, 0) for t in (r, w, k, v, kk, kk * a))

    def step(S, inp):
        r_t, w_t, k_t, v_t, kk_t, b_t = inp
        s_kk = jnp.einsum('bhvk,bhk->bhv', S, kk_t)
        S = (S * w_t[:, :, None, :] - s_kk[..., None] * b_t[:, :, None, :]
             + v_t[..., None] * k_t[:, :, None, :])
        return S, jnp.einsum('bhvk,bhk->bhv', S, r_t)

    S, o = lax.scan(step, S0, xs)
    return S, jnp.moveaxis(o, 0, 1)


def _sb_block(q, segs, q_pos, sb_bias):
    z = jnp.concatenate([jnp.einsum('bqhd,bkhd->bhqk', q, k_s) for k_s, _, _ in segs],
                        axis=-1).astype(jnp.float32) * SB_SCALE
    z = z + sb_bias.astype(jnp.float32)[None, :, None, None]
    k_pos = jnp.concatenate([p for _, _, p in segs])
    mask = k_pos[None, :] < q_pos[:, None]
    log_1m_beta = jnp.where(mask, jax.nn.log_sigmoid(-z), 0.0)
    between = lax.cumsum(log_1m_beta, axis=3, reverse=True) - log_1m_beta
    weights = jnp.where(mask, jnp.exp(z + log_1m_beta + between), 0.0).astype(q.dtype)
    out = None
    off = 0
    for k_s, v_s, _ in segs:
        L = k_s.shape[1]
        part = jnp.einsum('bhqk,bkhd->bqhd', weights[..., off:off + L], v_s)
        out = part if out is None else out + part
        off += L
    return out


def _stick_breaking(q, k, v, past, sb_bias):
    T = q.shape[1]
    offset = 0 if past is None else past[0].shape[1]
    pos_new = offset + jnp.arange(T, dtype=jnp.int32)
    outs = []
    for s in range(0, T, Q_BLOCK):
        e = min(s + Q_BLOCK, T)
        segs = [(k[:, :e], v[:, :e], pos_new[:e])]
        if past is not None:
            segs = [(past[0], past[1], jnp.arange(offset, dtype=jnp.int32))] + segs
        outs.append(_sb_block(q[:, s:e], segs, pos_new[s:e], sb_bias))
    return jnp.concatenate(outs, axis=1)


def _moe(h, w_router, b_router, w_gate_up, b_gate_up, w_down, b_down):
    M, D = h.shape
    n_assign = M * TOP_K
    logits = (h @ w_router).astype(jnp.float32) + b_router.astype(jnp.float32)
    top_logit, top_e = lax.top_k(logits, TOP_K)
    gate = jax.nn.softmax(top_logit, axis=-1).reshape(n_assign)
    e_flat = top_e.reshape(n_assign)
    tok_flat = jnp.arange(n_assign, dtype=jnp.int32) // TOP_K
    order = jnp.argsort(e_flat)
    e_sorted = e_flat[order]
    tok_sorted = tok_flat[order]
    gate_sorted = gate[order]
    counts = jnp.bincount(e_flat, length=N_EXPERTS)
    start = jnp.cumsum(counts) - counts
    padded = (counts + MOE_BLOCK - 1) // MOE_BLOCK * MOE_BLOCK
    padded_end = jnp.cumsum(padded)
    dest = (padded_end[e_sorted] - padded[e_sorted]
            + jnp.arange(n_assign, dtype=jnp.int32) - start[e_sorted])
    n_blocks = -(-(n_assign + N_EXPERTS * (MOE_BLOCK - 1)) // MOE_BLOCK)
    row_tok = jnp.full((n_blocks * MOE_BLOCK,), M, jnp.int32).at[dest].set(tok_sorted)
    block_e = jnp.minimum(jnp.searchsorted(padded_end, jnp.arange(n_blocks) * MOE_BLOCK,
                                           side='right'), N_EXPERTS - 1)
    h_pad = jnp.concatenate([h, jnp.zeros((1, D), h.dtype)], axis=0)

    def expert_block(args):
        rows, e = args
        xb = h_pad[rows]
        gu = xb @ w_gate_up[e] + b_gate_up[e]
        g_lin, u_lin = jnp.split(gu, 2, axis=-1)
        g_lin = jnp.minimum(g_lin, SWIGLU_LIMIT)
        u_lin = jnp.clip(u_lin, -SWIGLU_LIMIT, SWIGLU_LIMIT)
        act = g_lin * jax.nn.sigmoid(SWIGLU_ALPHA * g_lin) * (u_lin + 1.0)
        return act @ w_down[e] + b_down[e]

    y_rows = lax.map(expert_block, (row_tok.reshape(n_blocks, MOE_BLOCK), block_e)).reshape(-1, D)
    y_assign = y_rows[dest] * gate_sorted[:, None].astype(h.dtype)
    return jax.ops.segment_sum(y_assign, tok_sorted, num_segments=M)


def _hybrid_layer(x, shift0, wkv0, past, params):
    (norm1_g, w_in, mu_shift, w0, w_lora_up, a0, a_lora_up, k_k, k_a, r_k, ln_x_g, ln_x_b,
     q_norm_g, k_norm_g, sb_bias, w_out, norm2_g, w_router, b_router, w_gate_up, b_gate_up,
     w_down, b_down) = params
    f32 = jnp.float32
    B, T, _ = x.shape
    h = _rmsnorm(x, norm1_g)
    proj = h @ w_in
    p_rw, p_q, p_k, p_v, gate_a, gate_b = jnp.split(proj, IN_SPLITS, axis=-1)

    prev = jnp.concatenate([shift0[:, None, :].astype(p_rw.dtype), p_rw[:, :-1]], axis=1)
    xm = p_rw + (prev - p_rw) * mu_shift
    r, k, v, wd, ad = jnp.split(xm, RW_SPLITS, axis=-1)
    w_raw = (w0 + jnp.tanh(wd) @ w_lora_up).astype(f32)
    decay = jnp.exp(-jnp.exp(-jax.nn.softplus(-w_raw) - DECAY_OFFSET))
    a = jax.nn.sigmoid((a0 + ad @ a_lora_up).astype(f32))
    heads = lambda t: t.astype(f32).reshape(B, T, RW_HEADS, HEAD_DIM)
    kk = heads(k * k_k)
    kk = kk / jnp.maximum(jnp.linalg.norm(kk, axis=-1, keepdims=True), 1e-12)
    k_h = heads(k.astype(f32) * (1.0 + (a - 1.0) * k_a.astype(f32)))
    r_h = heads(r)
    v_h = heads(v)
    wkv, o = _wkv7_scan(wkv0.astype(f32), r_h, heads(decay), k_h, v_h, kk, heads(a))
    o_c = o - jnp.mean(o, axis=-1, keepdims=True)
    o = o_c * lax.rsqrt(jnp.mean(o_c * o_c, axis=-1, keepdims=True) + GN_EPS)
    o = o.reshape(B, T, RW_WIDTH) * ln_x_g.astype(f32) + ln_x_b.astype(f32)
    bonus = jnp.sum(r_h * k_h * r_k.astype(f32), axis=-1, keepdims=True) * v_h
    o_a = (o + bonus.reshape(B, T, RW_WIDTH)).astype(x.dtype) * jax.nn.sigmoid(gate_a)

    q = _rmsnorm(p_q.reshape(B, T, SB_HEADS, HEAD_DIM), q_norm_g)
    kb = _rmsnorm(p_k.reshape(B, T, SB_HEADS, HEAD_DIM), k_norm_g)
    vb = p_v.reshape(B, T, SB_HEADS, HEAD_DIM)
    o_b = _stick_breaking(q, kb, vb, past, sb_bias).reshape(B, T, SB_WIDTH) * jax.nn.sigmoid(gate_b)

    x = x + jnp.concatenate([o_a, o_b], axis=-1) @ w_out

    h2 = _rmsnorm(x, norm2_g).reshape(B * T, D_MODEL)
    y = x + _moe(h2, w_router, b_router, w_gate_up, b_gate_up, w_down, b_down).reshape(B, T, D_MODEL)
    return y, kb, vb, wkv.astype(x.dtype), p_rw[:, -1]


def setup_inputs(seed: int = 0) -> dict:
    key = jax.random.key(seed)
    ks = jax.random.split(key, 32)
    f32 = jnp.float32

    def nrm(k, shape, scale):
        return jax.random.normal(k, shape, f32) * scale

    n_pages = PAST_LEN // PAGE_SIZE
    n_used = DEC_BATCH * n_pages
    n_pool = n_used + max(1, n_used // 4)
    page_table = jax.random.permutation(ks[2], n_pool)[:n_used].reshape(DEC_BATCH, n_pages).astype(jnp.int32)
    return {
        'x_prompt': nrm(ks[0], (BATCH, SEQ, D_MODEL), 1.0),
        'x_sample': nrm(ks[1], (DEC_BATCH, DEC_SEQ, D_MODEL), 1.0),
        'cache_k': nrm(ks[3], (n_pool, PAGE_SIZE, SB_HEADS, HEAD_DIM), 1.0),
        'cache_v': nrm(ks[4], (n_pool, PAGE_SIZE, SB_HEADS, HEAD_DIM), 1.0),
        'page_table': page_table,
        'state_wkv': nrm(ks[5], (DEC_BATCH, RW_HEADS, HEAD_DIM, HEAD_DIM), 0.5),
        'state_shift': nrm(ks[6], (DEC_BATCH, RW_PROJ), 1.0),
        'norm1_g': 1.0 + nrm(ks[7], (D_MODEL,), 0.05),
        'w_in': nrm(ks[8], (D_MODEL, IN_WIDTH), D_MODEL ** -0.5),
        'mu_shift': jax.random.uniform(ks[9], (RW_PROJ,), f32, 0.1, 0.9),
        'w0': jax.random.uniform(ks[10], (RW_WIDTH,), f32, -6.0, 1.0),
        'w_lora_up': nrm(ks[11], (W_RANK, RW_WIDTH), 0.5 * W_RANK ** -0.5),
        'a0': nrm(ks[12], (RW_WIDTH,), 0.2),
        'a_lora_up': nrm(ks[13], (A_RANK, RW_WIDTH), A_RANK ** -0.5),
        'k_k': 0.85 + nrm(ks[14], (RW_WIDTH,), 0.05),
        'k_a': 1.0 + nrm(ks[15], (RW_WIDTH,), 0.05),
        'r_k': nrm(ks[16], (RW_HEADS, HEAD_DIM), 0.1),
        'ln_x_g': 1.0 + nrm(ks[17], (RW_WIDTH,), 0.05),
        'ln_x_b': nrm(ks[18], (RW_WIDTH,), 0.02),
        'q_norm_g': 1.0 + nrm(ks[19], (HEAD_DIM,), 0.05),
        'k_norm_g': 1.0 + nrm(ks[20], (HEAD_DIM,), 0.05),
        'sb_bias': SB_BIAS_INIT + nrm(ks[29], (SB_HEADS,), 0.1),
        'w_out': nrm(ks[21], (MIX_WIDTH, D_MODEL), MIX_WIDTH ** -0.5),
        'norm2_g': 1.0 + nrm(ks[22], (D_MODEL,), 0.05),
        'w_router': nrm(ks[23], (D_MODEL, N_EXPERTS), D_MODEL ** -0.5),
        'b_router': nrm(ks[24], (N_EXPERTS,), 0.01),
        'w_gate_up': nrm(ks[25], (N_EXPERTS, D_MODEL, 2 * D_FF), D_MODEL ** -0.5),
        'b_gate_up': nrm(ks[26], (N_EXPERTS, 2 * D_FF), 0.02),
        'w_down': nrm(ks[27], (N_EXPERTS, D_FF, D_MODEL), D_FF ** -0.5),
        'b_down': nrm(ks[28], (N_EXPERTS, D_MODEL), 0.02),
    }


def reference(x_prompt, x_sample, cache_k, cache_v, page_table, state_wkv, state_shift,
              norm1_g, w_in, mu_shift, w0, w_lora_up, a0, a_lora_up, k_k, k_a, r_k,
              ln_x_g, ln_x_b, q_norm_g, k_norm_g, sb_bias, w_out, norm2_g, w_router, b_router,
              w_gate_up, b_gate_up, w_down, b_down):
    params = (norm1_g, w_in, mu_shift, w0, w_lora_up, a0, a_lora_up, k_k, k_a, r_k,
              ln_x_g, ln_x_b, q_norm_g, k_norm_g, sb_bias, w_out, norm2_g, w_router, b_router,
              w_gate_up, b_gate_up, w_down, b_down)
    b_p = x_prompt.shape[0]
    b_s = x_sample.shape[0]

    y_p, k_p, v_p, wkv_p, sh_p = _hybrid_layer(
        x_prompt, jnp.zeros((b_p, RW_PROJ), x_prompt.dtype),
        jnp.zeros((b_p, RW_HEADS, HEAD_DIM, HEAD_DIM), jnp.float32), None, params)

    past_len = page_table.shape[1] * PAGE_SIZE
    k_past = cache_k[page_table].reshape(b_s, past_len, SB_HEADS, HEAD_DIM)
    v_past = cache_v[page_table].reshape(b_s, past_len, SB_HEADS, HEAD_DIM)
    y_s, k_s, v_s, wkv_s, sh_s = _hybrid_layer(
        x_sample, state_shift, state_wkv, (k_past, v_past), params)

    return (y_p, y_s, k_p, v_p, k_s, v_s, wkv_p, wkv_s, sh_p, sh_s)
```

```python
import functools
import math

import jax
import jax.numpy as jnp
from jax import lax
from jax.experimental import pallas as pl
from jax.experimental.pallas import tpu as pltpu

F32 = jnp.float32
BF16 = jnp.bfloat16

HEAD_DIM = 64
LANES = 128
PAGE_SIZE = 128
W_RANK = 64
A_RANK = 64
DECAY_SCALE = math.exp(-0.5)
GN_EPS = 64e-5
RMS_EPS = 1e-6
SB_SCALE = HEAD_DIM ** -0.5
TOP_K = 4
SWIGLU_ALPHA = 1.702
SWIGLU_LIMIT = 7.0
VMEM_LIMIT = 56 * 1024 * 1024

NN = (((1,), (0,)), ((), ()))
NT = (((1,), (1,)), ((), ()))
TN = (((0,), (0,)), ((), ()))


def _dot(a, b, dims=NN):
    return lax.dot_general(a, b, dims, preferred_element_type=F32)


def _split2(x):
    hi = x.astype(BF16)
    lo = (x - hi.astype(F32)).astype(BF16)
    return hi, lo


def _split3(x):
    hi = x.astype(BF16)
    r1 = x - hi.astype(F32)
    mid = r1.astype(BF16)
    lo = (r1 - mid.astype(F32)).astype(BF16)
    return hi, mid, lo


def _dot3(a, b, dims=NN):
    (ah, al), (bh, bl) = a, b
    return _dot(ah, bh, dims) + (_dot(ah, bl, dims) + _dot(al, bh, dims))


def _ones_dot(m, x):
    hi, mid, lo = _split3(x)
    return _dot(m, hi) + (_dot(m, mid) + _dot(m, lo))


def _pair_ones():
    r = lax.broadcasted_iota(jnp.int32, (LANES, LANES), 0) // HEAD_DIM
    c = lax.broadcasted_iota(jnp.int32, (LANES, LANES), 1) // HEAD_DIM
    return (r == c).astype(BF16)


def _head_sum(x, bd):
    hi, lo = _split2(x)
    parts = []
    for p in range(x.shape[1] // LANES):
        sl = slice(LANES * p, LANES * (p + 1))
        parts.append(_dot(hi[:, sl], bd) + _dot(lo[:, sl], bd))
    return parts[0] if len(parts) == 1 else jnp.concatenate(parts, axis=1)


def _sigmoid(x):
    return 1.0 / (1.0 + jnp.exp(-x))


def _in_proj_kernel(x_ref, g_ref, w_ref, qg_ref, kg_ref,
                    prw_ref, q16_ref, kb_ref, k16_ref, vb_ref, v16_ref, ga_ref, gb_ref, *, rw_proj, width):
    x = x_ref[...]
    ms = jnp.mean(x * x, axis=-1, keepdims=True)
    h = ((x * lax.rsqrt(ms + RMS_EPS)) * g_ref[...]).astype(BF16)
    bd = _pair_ones()
    o = rw_proj
    prw_ref[...] = _dot(h, w_ref[:, 0:o])
    q = _dot(h, w_ref[:, o:o + width])
    qn = (q * lax.rsqrt(_head_sum(q * q, bd) * (1.0 / HEAD_DIM) + RMS_EPS)) * qg_ref[...]
    q16_ref[...] = (qn * SB_SCALE).astype(BF16)
    k = _dot(h, w_ref[:, o + width:o + 2 * width])
    kn = (k * lax.rsqrt(_head_sum(k * k, bd) * (1.0 / HEAD_DIM) + RMS_EPS)) * kg_ref[...]
    kb_ref[...] = kn
    k16_ref[...] = kn.astype(BF16)
    v = _dot(h, w_ref[:, o + 2 * width:o + 3 * width])
    vb_ref[...] = v
    v16_ref[...] = v.astype(BF16)
    ga_ref[...] = _dot(h, w_ref[:, o + 3 * width:o + 4 * width])
    gb_ref[...] = _dot(h, w_ref[:, o + 4 * width:o + 5 * width])


def _in_proj(x2d, norm_g, w_in16, qg, kg, *, rw_proj, width, tm):
    m, d = x2d.shape
    tm = min(tm, m)
    row = lambda n: pl.BlockSpec((tm, n), lambda i: (i, 0))
    full = lambda a: pl.BlockSpec(a.shape, lambda i: (0,) * a.ndim)
    out_shapes = (
        jax.ShapeDtypeStruct((m, rw_proj), F32),
        jax.ShapeDtypeStruct((m, width), BF16),
        jax.ShapeDtypeStruct((m, width), F32), jax.ShapeDtypeStruct((m, width), BF16),
        jax.ShapeDtypeStruct((m, width), F32), jax.ShapeDtypeStruct((m, width), BF16),
        jax.ShapeDtypeStruct((m, width), F32), jax.ShapeDtypeStruct((m, width), F32),
    )
    return pl.pallas_call(
        functools.partial(_in_proj_kernel, rw_proj=rw_proj, width=width),
        grid=(m // tm,),
        in_specs=[row(d), full(norm_g), full(w_in16), full(qg), full(kg)],
        out_specs=tuple(row(s.shape[1]) for s in out_shapes),
        out_shape=out_shapes,
        compiler_params=pltpu.CompilerParams(dimension_semantics=("arbitrary",), vmem_limit_bytes=VMEM_LIMIT),
        name="in_proj",
    )(x2d, norm_g, w_in16, qg, kg)


def _rwkv_kernel(p_ref, ga_ref, sh0_ref, s0_ref, mu_ref, w0_ref, a0_ref, lora_ref, kk_ref, ka_ref, rk_ref,
                 lng_ref, lnb_ref, mix_ref, sout_ref,
                 carry, state, kap_s, bet_s, gam_s, rho_s, betc_s, gamc_s, v_s, wc_s, o_s, *, chunk, width):
    tb = pl.program_id(1)
    tile = p_ref.shape[1]
    n_pairs = width // LANES

    @pl.when(tb == 0)
    def _():
        carry[...] = sh0_ref[0]
        state[...] = s0_ref[0]

    p = p_ref[0]
    rows = lax.broadcasted_iota(jnp.int32, p.shape, 0)
    prev = jnp.where(rows == 0, carry[...], pltpu.roll(p, 1, axis=0))
    carry[...] = p[tile - 1:tile, :]
    xm = p + (prev - p) * mu_ref[...]
    r = xm[:, 0:width]
    k = xm[:, width:2 * width]
    v = xm[:, 2 * width:3 * width]
    wa = xm[:, 3 * width:3 * width + W_RANK + A_RANK]
    lane = lax.broadcasted_iota(jnp.int32, wa.shape, 1)
    lora_in = jnp.where(lane < W_RANK, jnp.tanh(wa), wa)
    lora = _dot3(_split2(lora_in), _split2(lora_ref[...]))
    lw = -DECAY_SCALE * _sigmoid(w0_ref[...] + lora[:, 0:width])
    a = _sigmoid(a0_ref[...] + lora[:, width:2 * width])
    bd = _pair_ones()
    kk = k * kk_ref[...]
    kk = kk / jnp.maximum(jnp.sqrt(_head_sum(kk * kk, bd)), 1e-12)
    k_h = k * (1.0 + (a - 1.0) * ka_ref[...])
    b = kk * a
    bonus = _head_sum(r * k_h * rk_ref[...], bd) * v

    ti = lax.broadcasted_iota(jnp.int32, (tile, tile), 0)
    tj = lax.broadcasted_iota(jnp.int32, (tile, tile), 1)
    same = (ti // chunk) == (tj // chunk)
    cum = _ones_dot((same & (ti >= tj)).astype(BF16), lw)
    tot = _ones_dot(same.astype(BF16), lw)
    e_neg = jnp.exp(-cum)
    e_c = jnp.exp(tot - cum)
    kap_s[...] = kk * jnp.exp(cum - lw)
    bet_s[...] = b * e_neg
    gam_s[...] = k_h * e_neg
    rho_s[...] = r * jnp.exp(cum)
    betc_s[...] = b * e_c
    gamc_s[...] = k_h * e_c
    v_s[...] = v
    wc_s[...] = jnp.exp(tot)

    ci_ = lax.broadcasted_iota(jnp.int32, (chunk, chunk), 0)
    cj_ = lax.broadcasted_iota(jnp.int32, (chunk, chunk), 1)
    strict = ci_ > cj_
    incl = ci_ >= cj_
    lane_c = lax.broadcasted_iota(jnp.int32, (chunk, LANES), 1)
    head0 = lane_c < HEAD_DIM
    bd_mask = _pair_ones() > 0
    n_double = int(math.log2(chunk)) - 1

    def chunk_body(ci, carry_):
        rs = pl.ds(pl.multiple_of(ci * chunk, chunk), chunk)
        for pr in range(n_pairs):
            ls = slice(LANES * pr, LANES * (pr + 1))
            kap = kap_s[rs, ls]
            rho = rho_s[rs, ls]
            bet = _split2(bet_s[rs, ls])
            gam = _split2(gam_s[rs, ls])
            vv = v_s[rs, ls]
            vsp = _split2(vv)
            s_bd = state[pr]
            ssp = _split2(s_bd)
            x0 = _dot3(_split2(kap), ssp, NT)
            o0 = _dot3(_split2(rho), ssp, NT)
            u = jnp.zeros((chunk, LANES), F32)
            o = jnp.zeros((chunk, LANES), F32)
            for h in range(2):
                mh = head0 if h == 0 else jnp.logical_not(head0)
                kap_h = _split2(jnp.where(mh, kap, 0.0))
                rho_h = _split2(jnp.where(mh, rho, 0.0))
                a1 = jnp.where(strict, _dot3(kap_h, bet, NT), 0.0)
                a2 = jnp.where(strict, _dot3(kap_h, gam, NT), 0.0)
                rb = jnp.where(incl, _dot3(rho_h, bet, NT), 0.0)
                rg = jnp.where(incl, _dot3(rho_h, gam, NT), 0.0)
                rhs = _dot3(_split2(a2), vsp)
                rhs = rhs + x0
                ap = _split2(a1)
                x = rhs - _dot3(ap, _split2(rhs))
                for _ in range(n_double):
                    ap = _split2(_dot3(ap, ap))
                    x = x + _dot3(ap, _split2(x))
                u_h = -x
                o_h = _dot3(_split2(rb), _split2(u_h)) + _dot3(_split2(rg), vsp)
                u = jnp.where(mh, u_h, u)
                o = jnp.where(mh, o_h, o)
            o_s[rs, ls] = o + o0
            wc = wc_s[rs, ls][0:1, :]
            s_new = (s_bd * wc + _dot3(_split2(u), _split2(betc_s[rs, ls]), TN)
                     + _dot3(vsp, _split2(gamc_s[rs, ls]), TN))
            state[pr] = jnp.where(bd_mask, s_new, 0.0)
        return carry_

    lax.fori_loop(0, tile // chunk, chunk_body, 0)

    o = o_s[...]
    mean = _head_sum(o, bd) * (1.0 / HEAD_DIM)
    oc = o - mean
    var = _head_sum(oc * oc, bd) * (1.0 / HEAD_DIM)
    on = (oc * lax.rsqrt(var + GN_EPS)) * lng_ref[...] + lnb_ref[...]
    mix_ref[0] = ((on + bonus) * _sigmoid(ga_ref[0])).astype(BF16)

    @pl.when(tb == pl.num_programs(1) - 1)
    def _():
        sout_ref[0] = state[...]


def _rwkv(p_rw, gate_a, shift0, s0_bd, mu, w0, a0, lora_up, k_k, k_a, r_k, ln_g, ln_b, *, tile, chunk):
    bsz, t, rw_proj = p_rw.shape
    width = gate_a.shape[-1]
    n_pairs = width // LANES
    tok = lambda n: pl.BlockSpec((1, tile, n), lambda b, i: (b, i, 0))
    full = lambda a: pl.BlockSpec(a.shape, lambda b, i: (0,) * a.ndim)
    per_b = lambda a: pl.BlockSpec((1,) + a.shape[1:], lambda b, i: (b,) + (0,) * (a.ndim - 1))
    scr = lambda: pltpu.VMEM((tile, width), F32)
    return pl.pallas_call(
        functools.partial(_rwkv_kernel, chunk=chunk, width=width),
        grid=(bsz, t // tile),
        in_specs=[tok(rw_proj), tok(width), per_b(shift0), per_b(s0_bd), full(mu), full(w0), full(a0),
                  full(lora_up), full(k_k), full(k_a), full(r_k), full(ln_g), full(ln_b)],
        out_specs=(tok(width), per_b(s0_bd)),
        out_shape=(jax.ShapeDtypeStruct((bsz, t, width), BF16), jax.ShapeDtypeStruct(s0_bd.shape, F32)),
        scratch_shapes=[pltpu.VMEM((1, rw_proj), F32), pltpu.VMEM((n_pairs, LANES, LANES), F32)]
                       + [scr() for _ in range(9)],
        compiler_params=pltpu.CompilerParams(dimension_semantics=("arbitrary", "arbitrary"),
                                             vmem_limit_bytes=VMEM_LIMIT),
        name="rwkv",
    )(p_rw, gate_a, shift0, s0_bd, mu, w0, a0, lora_up, k_k, k_a, r_k, ln_g, ln_b)


def _sb_block(z, c, tri, valid):
    lg = -(jnp.maximum(z, 0.0) + jnp.log(1.0 + jnp.exp(-jnp.abs(z))))
    if valid is not None:
        lg = jnp.where(valid, lg, 0.0)
    hi, lo = _split2(lg)
    incl = _dot(hi, tri) + _dot(lo, tri)
    w = jnp.exp(z + incl + c)
    if valid is not None:
        w = jnp.where(valid, w, 0.0)
    return w.astype(BF16), c + incl[:, 0:1]


def _suffix_tri(n):
    r = lax.broadcasted_iota(jnp.int32, (n, n), 0)
    c = lax.broadcasted_iota(jnp.int32, (n, n), 1)
    return (r >= c).astype(BF16)


def _sb_prompt_kernel(bias_ref, q_ref, k_ref, v_ref, gb_ref, o_ref, acc_ref, *, tq, tk):
    hp = pl.program_id(1)
    qi = pl.program_id(2)
    q = q_ref[0]
    lane = lax.broadcasted_iota(jnp.int32, (tq, LANES), 1)
    head0 = lane < HEAD_DIM
    tri = _suffix_tri(tk)
    row = lax.broadcasted_iota(jnp.int32, (tq, tk), 0)
    col = lax.broadcasted_iota(jnp.int32, (tq, tk), 1)
    n_diag = tq // tk
    out = jnp.zeros((tq, LANES), F32)
    for h in range(2):
        qh = jnp.where(head0 if h == 0 else jnp.logical_not(head0), q, jnp.zeros_like(q))
        bias = bias_ref[2 * hp + h]
        acc_ref[...] = jnp.zeros_like(acc_ref)

        def step(kb, c, masked):
            ks = pl.multiple_of(kb * tk, tk)
            kblk = k_ref[0, pl.ds(ks, tk), :]
            vblk = v_ref[0, pl.ds(ks, tk), :]
            z = _dot(qh, kblk, NT) + bias
            valid = (col + (kb * tk - qi * tq)) < row if masked else None
            w, c = _sb_block(z, c, tri, valid)
            acc_ref[...] += _dot(w, vblk)
            return c

        last = (qi + 1) * n_diag - 1
        c = jnp.zeros((tq, 1), F32)
        c = lax.fori_loop(0, n_diag, lambda j, c: step(last - j, c, True), c)
        c = lax.fori_loop(0, qi * n_diag, lambda j, c: step(qi * n_diag - 1 - j, c, False), c)
        out = jnp.where(head0 if h == 0 else jnp.logical_not(head0), acc_ref[...], out)
    o_ref[0] = (out * _sigmoid(gb_ref[0])).astype(BF16)


def _sb_prompt(q16, k16, v16, gate_b, sb_bias, *, tq, tk):
    bsz, t, width = q16.shape
    qspec = pl.BlockSpec((1, tq, LANES), lambda b, hp, i, bias: (b, i, hp))
    kspec = pl.BlockSpec((1, t, LANES), lambda b, hp, i, bias: (b, 0, hp))
    return pl.pallas_call(
        functools.partial(_sb_prompt_kernel, tq=tq, tk=tk),
        grid_spec=pltpu.PrefetchScalarGridSpec(
            num_scalar_prefetch=1,
            grid=(bsz, width // LANES, t // tq),
            in_specs=[qspec, kspec, kspec, qspec],
            out_specs=qspec,
            scratch_shapes=[pltpu.VMEM((tq, LANES), F32)],
        ),
        out_shape=jax.ShapeDtypeStruct((bsz, t, width), BF16),
        compiler_params=pltpu.CompilerParams(dimension_semantics=("arbitrary",) * 3, vmem_limit_bytes=VMEM_LIMIT),
        name="sb_prompt",
    )(sb_bias, q16, k16, v16, gate_b)


def _sb_sample_kernel(pt_ref, q_ref, bias_ref, kn_ref, vn_ref, *rest, n_q, pages_per_step, n_heads):
    kp_refs = rest[:pages_per_step]
    vp_refs = rest[pages_per_step:2 * pages_per_step]
    gb_ref, o_ref, qbd_ref, acc_ref, c_ref = rest[2 * pages_per_step:]
    j = pl.program_id(1)
    n_rows = n_heads * n_q
    width = n_heads * HEAD_DIM
    row_head = lax.broadcasted_iota(jnp.int32, (n_rows, width), 0) // n_q
    lane_head = lax.broadcasted_iota(jnp.int32, (n_rows, width), 1) // HEAD_DIM
    own = row_head == lane_head
    tri = _suffix_tri(PAGE_SIZE)

    def block(kblk, vblk, valid):
        z = _dot(qbd_ref[...], kblk.astype(BF16), NT) + bias_ref[...]
        w, c = _sb_block(z, c_ref[...], tri, valid)
        acc_ref[...] += _dot(w, vblk.astype(BF16))
        c_ref[...] = c

    @pl.when(j == 0)
    def _():
        q = q_ref[0]
        qt = jnp.concatenate([q] * n_heads, axis=0)
        qbd_ref[...] = jnp.where(own, qt, jnp.zeros_like(qt))
        acc_ref[...] = jnp.zeros_like(acc_ref)
        c_ref[...] = jnp.zeros_like(c_ref)
        tq = lax.broadcasted_iota(jnp.int32, (n_rows, PAGE_SIZE), 0) % n_q
        ks = lax.broadcasted_iota(jnp.int32, (n_rows, PAGE_SIZE), 1)
        block(kn_ref[0], vn_ref[0], ks < tq)

    @pl.when(j > 0)
    def _():
        for p in range(pages_per_step):
            block(kp_refs[p][0], vp_refs[p][0], None)

    @pl.when(j == pl.num_programs(1) - 1)
    def _():
        a = jnp.where(own, acc_ref[...], 0.0).reshape(n_heads, n_q, width)
        o_ref[0] = (jnp.sum(a, axis=0) * _sigmoid(gb_ref[0])).astype(BF16)


def _sb_sample(q16, k_new, v_new, gate_b, cache_k, cache_v, page_table, bias_rows, *, pages_per_step):
    bsz, n_q, width = q16.shape
    n_heads = width // HEAD_DIM
    n_pages = page_table.shape[1]
    n_steps = n_pages // pages_per_step

    def page_spec(p):
        def index(b, j, pt):
            step = jnp.maximum(j - 1, 0)
            return (pt[b, n_pages - 1 - (step * pages_per_step + p)], 0, 0)
        return pl.BlockSpec((1, PAGE_SIZE, width), index)

    seq = lambda n: pl.BlockSpec((1, n, width), lambda b, j, pt: (b, 0, 0))
    return pl.pallas_call(
        functools.partial(_sb_sample_kernel, n_q=n_q, pages_per_step=pages_per_step, n_heads=n_heads),
        grid_spec=pltpu.PrefetchScalarGridSpec(
            num_scalar_prefetch=1,
            grid=(bsz, n_steps + 1),
            in_specs=[seq(n_q), pl.BlockSpec(bias_rows.shape, lambda b, j, pt: (0, 0)),
                      seq(PAGE_SIZE), seq(PAGE_SIZE)]
                     + [page_spec(p) for p in range(pages_per_step)] * 2 + [seq(n_q)],
            out_specs=seq(n_q),
            scratch_shapes=[pltpu.VMEM((n_heads * n_q, width), BF16), pltpu.VMEM((n_heads * n_q, width), F32),
                            pltpu.VMEM((n_heads * n_q, 1), F32)],
        ),
        out_shape=jax.ShapeDtypeStruct((bsz, n_q, width), BF16),
        compiler_params=pltpu.CompilerParams(dimension_semantics=("arbitrary", "arbitrary"),
                                             vmem_limit_bytes=VMEM_LIMIT),
        name="sb_sample",
    )(page_table, q16, bias_rows, k_new, v_new, *([cache_k] * pages_per_step), *([cache_v] * pages_per_step),
      gate_b)


def _out_proj_kernel(x_ref, ma_ref, mb_ref, w_ref, g_ref, wr_ref, br_ref, x1_ref, h2_ref, lg_ref):
    half = ma_ref.shape[1]
    x1 = x_ref[...] + (_dot(ma_ref[...], w_ref[0:half, :]) + _dot(mb_ref[...], w_ref[half:2 * half, :]))
    x1_ref[...] = x1
    ms = jnp.mean(x1 * x1, axis=-1, keepdims=True)
    h2 = (x1 * lax.rsqrt(ms + RMS_EPS)) * g_ref[...]
    h2_ref[...] = h2.astype(BF16)
    lg_ref[...] = _dot3(_split2(h2), _split2(wr_ref[...])) + br_ref[...]


def _out_proj(x2d, mix_a, mix_b, w_out16, norm_g, w_router_pad, b_router_pad, *, tm):
    m, d = x2d.shape
    tm = min(tm, m)
    row = lambda n: pl.BlockSpec((tm, n), lambda i: (i, 0))
    full = lambda a: pl.BlockSpec(a.shape, lambda i: (0,) * a.ndim)
    n_lg = w_router_pad.shape[1]
    return pl.pallas_call(
        _out_proj_kernel,
        grid=(m // tm,),
        in_specs=[row(d), row(mix_a.shape[1]), row(mix_b.shape[1]), full(w_out16), full(norm_g),
                  full(w_router_pad), full(b_router_pad)],
        out_specs=(row(d), row(d), row(n_lg)),
        out_shape=(jax.ShapeDtypeStruct((m, d), F32), jax.ShapeDtypeStruct((m, d), BF16),
                   jax.ShapeDtypeStruct((m, n_lg), F32)),
        compiler_params=pltpu.CompilerParams(dimension_semantics=("arbitrary",), vmem_limit_bytes=VMEM_LIMIT),
        name="out_proj",
    )(x2d, mix_a, mix_b, w_out16, norm_g, w_router_pad, b_router_pad)


def _moe_kernel(te_ref, nu_ref, x_ref, gate_ref, wgu_ref, bgu_ref, wd_ref, bd_ref, o_ref):
    t = pl.program_id(0)
    d_ff = wd_ref.shape[1]

    @pl.when(t < nu_ref[0])
    def _():
        gu = _dot(x_ref[...], wgu_ref[0]) + bgu_ref[0]
        g_lin = jnp.minimum(gu[:, 0:d_ff], SWIGLU_LIMIT)
        u_lin = jnp.clip(gu[:, d_ff:2 * d_ff], -SWIGLU_LIMIT, SWIGLU_LIMIT)
        act = g_lin * _sigmoid(SWIGLU_ALPHA * g_lin) * (u_lin + 1.0)
        y = _dot(act.astype(BF16), wd_ref[0]) + bd_ref[0]
        o_ref[...] = y * gate_ref[...]

    @pl.when(t >= nu_ref[0])
    def _():
        o_ref[...] = jnp.zeros_like(o_ref)


def _moe(xs16, row_gate, tile_expert, n_used, w_gu16, b_gu, w_d16, b_d, *, tm):
    r, d = xs16.shape
    d_ff = w_d16.shape[1]
    n_tiles = r // tm
    row = lambda n: pl.BlockSpec((tm, n), lambda t, te, nu: (t, 0))
    ex = lambda a: pl.BlockSpec((1,) + a.shape[1:], lambda t, te, nu: (te[t], 0, 0))
    return pl.pallas_call(
        _moe_kernel,
        grid_spec=pltpu.PrefetchScalarGridSpec(
            num_scalar_prefetch=2,
            grid=(n_tiles,),
            in_specs=[row(d), row(1), ex(w_gu16), ex(b_gu), ex(w_d16), ex(b_d)],
            out_specs=row(d),
        ),
        out_shape=jax.ShapeDtypeStruct((r, d), F32),
        compiler_params=pltpu.CompilerParams(dimension_semantics=("arbitrary",), vmem_limit_bytes=VMEM_LIMIT),
        name="moe",
    )(tile_expert, n_used, xs16, row_gate, w_gu16, b_gu, w_d16, b_d)


def _route(logits, n_experts, tm):
    m = logits.shape[0]
    n_assign = m * TOP_K
    top_logit, top_e = lax.top_k(logits, TOP_K)
    gate = jax.nn.softmax(top_logit, axis=-1).reshape(n_assign)
    e_flat = top_e.reshape(n_assign).astype(jnp.int32)
    tok_flat = jnp.arange(n_assign, dtype=jnp.int32) // TOP_K
    onehot = (e_flat[:, None] == jnp.arange(n_experts, dtype=jnp.int32)[None, :]).astype(jnp.int32)
    csum = jnp.cumsum(onehot, axis=0)
    rank = jnp.sum((csum - onehot) * onehot, axis=1)
    counts = csum[-1]
    padded = (counts + tm - 1) // tm * tm
    padded_end = jnp.cumsum(padded)
    dest = (padded_end - padded)[e_flat] + rank
    n_tiles = (n_assign + n_experts * (tm - 1)) // tm + 1
    n_rows = n_tiles * tm
    row_tok = jnp.zeros((n_rows,), jnp.int32).at[dest].set(tok_flat)
    row_gate = jnp.zeros((n_rows,), F32).at[dest].set(gate)
    tile_expert = jnp.minimum(
        jnp.searchsorted(padded_end, jnp.arange(n_tiles, dtype=jnp.int32) * tm, side='right'),
        n_experts - 1).astype(jnp.int32)
    n_used = (padded_end[-1] // tm).astype(jnp.int32).reshape(1)
    return row_tok, row_gate[:, None], tile_expert, n_used, dest


def _pair_block_diag(s):
    bsz, h = s.shape[:2]
    s = s.reshape(bsz, h // 2, 2, HEAD_DIM, HEAD_DIM)
    eye = jnp.eye(2, dtype=s.dtype)
    return jnp.einsum('bpivk,ij->bpivjk', s, eye).reshape(bsz, h // 2, LANES, LANES)


def _pair_diag_blocks(s_bd):
    bsz, n_pairs = s_bd.shape[:2]
    s = s_bd.reshape(bsz, n_pairs, 2, HEAD_DIM, 2, HEAD_DIM)
    return jnp.stack([s[:, :, 0, :, 0, :], s[:, :, 1, :, 1, :]], axis=2).reshape(
        bsz, n_pairs * 2, HEAD_DIM, HEAD_DIM)


def kernel(x_prompt, x_sample, cache_k, cache_v, page_table, state_wkv, state_shift, norm1_g, w_in, mu_shift, w0, w_lora_up, a0, a_lora_up, k_k, k_a, r_k, ln_x_g, ln_x_b, q_norm_g, k_norm_g, sb_bias, w_out, norm2_g, w_router, b_router, w_gate_up, b_gate_up, w_down, b_down):
    d_model = x_prompt.shape[-1]
    rw_proj = mu_shift.shape[0]
    width = w0.shape[0]
    n_heads = width // HEAD_DIM
    n_experts = w_router.shape[1]
    row = lambda a: a.reshape(1, -1).astype(F32)

    w_in16 = w_in.astype(BF16)
    w_out16 = w_out.astype(BF16)
    w_gu16 = w_gate_up.astype(BF16)
    w_d16 = w_down.astype(BF16)
    qg = row(jnp.tile(q_norm_g, n_heads))
    kg = row(jnp.tile(k_norm_g, n_heads))
    lora_up = jnp.zeros((W_RANK + A_RANK, 2 * width), F32)
    lora_up = lora_up.at[:W_RANK, :width].set(w_lora_up).at[W_RANK:, width:].set(a_lora_up)
    w_router_pad = jnp.zeros((d_model, LANES), F32).at[:, :n_experts].set(w_router)
    b_router_pad = jnp.zeros((1, LANES), F32).at[0, :n_experts].set(b_router)
    rwkv_params = (row(mu_shift), row(w0), row(a0), lora_up, row(k_k), row(k_a), row(r_k), row(ln_x_g), row(ln_x_b))

    def front(x, shift0, s0, tile, chunk):
        bsz, t, _ = x.shape
        x2d = x.reshape(bsz * t, d_model)
        p_rw, q16, kb, k16, vb, v16, gate_a, gate_b = _in_proj(
            x2d, row(norm1_g), w_in16, qg, kg, rw_proj=rw_proj, width=width, tm=256)
        seq = lambda a: a.reshape(bsz, t, a.shape[-1])
        p_rw = seq(p_rw)
        mix_a, s_bd = _rwkv(p_rw, seq(gate_a), shift0[:, None, :], _pair_block_diag(s0), *rwkv_params,
                            tile=tile, chunk=chunk)
        return x2d, p_rw, seq(q16), kb, seq(k16), vb, seq(v16), seq(gate_b), mix_a, _pair_diag_blocks(s_bd)

    bp, tp, _ = x_prompt.shape
    bs, ts, _ = x_sample.shape
    xp2d, prw_p, q_p, kb_p, k16_p, vb_p, v16_p, gb_p, mixa_p, wkv_p = front(
        x_prompt, jnp.zeros((bp, rw_proj), F32), jnp.zeros((bp, n_heads, HEAD_DIM, HEAD_DIM), F32), 256, 64)
    xs2d, prw_s, q_s, kb_s, k16_s, vb_s, v16_s, gb_s, mixa_s, wkv_s = front(
        x_sample, state_shift, state_wkv, ts, ts)

    mixb_p = _sb_prompt(q_p, k16_p, v16_p, gb_p, sb_bias.astype(F32), tq=256, tk=128)

    pad_keys = lambda a: jnp.pad(a, ((0, 0), (0, PAGE_SIZE - ts), (0, 0)))
    n_pool = cache_k.shape[0]
    mixb_s = _sb_sample(q_s, pad_keys(k16_s), pad_keys(v16_s), gb_s,
                        cache_k.reshape(n_pool, PAGE_SIZE, width), cache_v.reshape(n_pool, PAGE_SIZE, width),
                        page_table.astype(jnp.int32), jnp.repeat(sb_bias.astype(F32), ts)[:, None],
                        pages_per_step=4)

    def back(x2d, mix_a, mix_b):
        m = x2d.shape[0]
        return _out_proj(x2d, mix_a.reshape(m, width), mix_b.reshape(m, width), w_out16, row(norm2_g),
                         w_router_pad, b_router_pad, tm=256)

    x1_p, h2_p, lg_p = back(xp2d, mixa_p, mixb_p)
    x1_s, h2_s, lg_s = back(xs2d, mixa_s, mixb_s)

    x1 = jnp.concatenate([x1_p, x1_s], axis=0)
    h2 = jnp.concatenate([h2_p, h2_s], axis=0)
    logits = jnp.concatenate([lg_p, lg_s], axis=0)[:, :n_experts]
    m_all = x1.shape[0]
    tm_moe = 512
    row_tok, row_gate, tile_expert, n_used, dest = _route(logits, n_experts, tm_moe)
    y_rows = _moe(h2[row_tok], row_gate, tile_expert, n_used, w_gu16, b_gate_up[:, None, :].astype(F32),
                  w_d16, b_down[:, None, :].astype(F32), tm=tm_moe)
    y = x1 + jnp.sum(y_rows[dest.reshape(m_all, TOP_K)], axis=1)

    mp = bp * tp
    hd = (n_heads, HEAD_DIM)
    return (y[:mp].reshape(bp, tp, d_model), y[mp:].reshape(bs, ts, d_model),
            kb_p.reshape(bp, tp, *hd), vb_p.reshape(bp, tp, *hd),
            kb_s.reshape(bs, ts, *hd), vb_s.reshape(bs, ts, *hd),
            wkv_p, wkv_s, prw_p[:, -1], prw_s[:, -1])
```

```python
import functools
import math

import jax
import jax.numpy as jnp
from jax import lax
from jax.experimental import pallas as pl
from jax.experimental.pallas import tpu as pltpu

F32 = jnp.float32
BF16 = jnp.bfloat16

HEAD_DIM = 64
LANES = 128
PAGE_SIZE = 128
W_RANK = 64
A_RANK = 64
DECAY_SCALE = math.exp(-0.5)
GN_EPS = 64e-5
RMS_EPS = 1e-6
SB_SCALE = HEAD_DIM ** -0.5
TOP_K = 4
SWIGLU_ALPHA = 1.702
SWIGLU_LIMIT = 7.0
VMEM_LIMIT = 56 * 1024 * 1024

NN = (((1,), (0,)), ((), ()))
NT = (((1,), (1,)), ((), ()))
TN = (((0,), (0,)), ((), ()))


def _dot(a, b, dims=NN):
    return lax.dot_general(a, b, dims, preferred_element_type=F32)


def _split2(x):
    hi = x.astype(BF16)
    lo = (x - hi.astype(F32)).astype(BF16)
    return hi, lo


def _split3(x):
    hi = x.astype(BF16)
    r1 = x - hi.astype(F32)
    mid = r1.astype(BF16)
    lo = (r1 - mid.astype(F32)).astype(BF16)
    return hi, mid, lo


def _dot3(a, b, dims=NN):
    (ah, al), (bh, bl) = a, b
    return _dot(ah, bh, dims) + (_dot(ah, bl, dims) + _dot(al, bh, dims))


def _ones_dot(m, x):
    hi, mid, lo = _split3(x)
    return _dot(m, hi) + (_dot(m, mid) + _dot(m, lo))


def _pair_ones():
    r = lax.broadcasted_iota(jnp.int32, (LANES, LANES), 0) // HEAD_DIM
    c = lax.broadcasted_iota(jnp.int32, (LANES, LANES), 1) // HEAD_DIM
    return (r == c).astype(BF16)


def _head_sum(x, bd):
    hi, lo = _split2(x)
    parts = []
    for p in range(x.shape[1] // LANES):
        sl = slice(LANES * p, LANES * (p + 1))
        parts.append(_dot(hi[:, sl], bd) + _dot(lo[:, sl], bd))
    return parts[0] if len(parts) == 1 else jnp.concatenate(parts, axis=1)


def _sigmoid(x):
    return 1.0 / (1.0 + jnp.exp(-x))


def _in_proj_kernel(x_ref, g_ref, w_ref, qg_ref, kg_ref,
                    prw_ref, q16_ref, kb_ref, k16_ref, vb_ref, v16_ref, ga_ref, gb_ref, *, rw_proj, width):
    x = x_ref[...]
    ms = jnp.mean(x * x, axis=-1, keepdims=True)
    h = ((x * lax.rsqrt(ms + RMS_EPS)) * g_ref[...]).astype(BF16)
    bd = _pair_ones()
    o = rw_proj
    prw_ref[...] = _dot(h, w_ref[:, 0:o])
    q = _dot(h, w_ref[:, o:o + width])
    qn = (q * lax.rsqrt(_head_sum(q * q, bd) * (1.0 / HEAD_DIM) + RMS_EPS)) * qg_ref[...]
    q16_ref[...] = (qn * SB_SCALE).astype(BF16)
    k = _dot(h, w_ref[:, o + width:o + 2 * width])
    kn = (k * lax.rsqrt(_head_sum(k * k, bd) * (1.0 / HEAD_DIM) + RMS_EPS)) * kg_ref[...]
    kb_ref[...] = kn
    k16_ref[...] = kn.astype(BF16)
    v = _dot(h, w_ref[:, o + 2 * width:o + 3 * width])
    vb_ref[...] = v
    v16_ref[...] = v.astype(BF16)
    ga_ref[...] = _dot(h, w_ref[:, o + 3 * width:o + 4 * width])
    gb_ref[...] = _dot(h, w_ref[:, o + 4 * width:o + 5 * width])


def _in_proj(x2d, norm_g, w_in16, qg, kg, *, rw_proj, width, tm):
    m, d = x2d.shape
    tm = min(tm, m)
    row = lambda n: pl.BlockSpec((tm, n), lambda i: (i, 0))
    full = lambda a: pl.BlockSpec(a.shape, lambda i: (0,) * a.ndim)
    out_shapes = (
        jax.ShapeDtypeStruct((m, rw_proj), F32),
        jax.ShapeDtypeStruct((m, width), BF16),
        jax.ShapeDtypeStruct((m, width), F32), jax.ShapeDtypeStruct((m, width), BF16),
        jax.ShapeDtypeStruct((m, width), F32), jax.ShapeDtypeStruct((m, width), BF16),
        jax.ShapeDtypeStruct((m, width), F32), jax.ShapeDtypeStruct((m, width), F32),
    )
    return pl.pallas_call(
        functools.partial(_in_proj_kernel, rw_proj=rw_proj, width=width),
        grid=(m // tm,),
        in_specs=[row(d), full(norm_g), full(w_in16), full(qg), full(kg)],
        out_specs=tuple(row(s.shape[1]) for s in out_shapes),
        out_shape=out_shapes,
        compiler_params=pltpu.CompilerParams(dimension_semantics=("arbitrary",), vmem_limit_bytes=VMEM_LIMIT),
        name="in_proj",
    )(x2d, norm_g, w_in16, qg, kg)


def _rwkv_kernel(p_ref, ga_ref, sh0_ref, s0_ref, mu_ref, w0_ref, a0_ref, lora_ref, kk_ref, ka_ref, rk_ref,
                 lng_ref, lnb_ref, mix_ref, sout_ref,
                 carry, state, kap_s, bet_s, gam_s, rho_s, betc_s, gamc_s, v_s, wc_s, o_s, *, chunk, width):
    tb = pl.program_id(1)
    tile = p_ref.shape[1]
    n_pairs = width // LANES

    @pl.when(tb == 0)
    def _():
        carry[...] = sh0_ref[0]
        state[...] = s0_ref[0]

    p = p_ref[0]
    rows = lax.broadcasted_iota(jnp.int32, p.shape, 0)
    prev = jnp.where(rows == 0, carry[...], pltpu.roll(p, 1, axis=0))
    carry[...] = p[tile - 1:tile, :]
    xm = p + (prev - p) * mu_ref[...]
    r = xm[:, 0:width]
    k = xm[:, width:2 * width]
    v = xm[:, 2 * width:3 * width]
    wa = xm[:, 3 * width:3 * width + W_RANK + A_RANK]
    lane = lax.broadcasted_iota(jnp.int32, wa.shape, 1)
    lora_in = jnp.where(lane < W_RANK, jnp.tanh(wa), wa)
    lora = _dot3(_split2(lora_in), _split2(lora_ref[...]))
    lw = -DECAY_SCALE * _sigmoid(w0_ref[...] + lora[:, 0:width])
    a = _sigmoid(a0_ref[...] + lora[:, width:2 * width])
    bd = _pair_ones()
    kk = k * kk_ref[...]
    kk = kk / jnp.maximum(jnp.sqrt(_head_sum(kk * kk, bd)), 1e-12)
    k_h = k * (1.0 + (a - 1.0) * ka_ref[...])
    b = kk * a
    bonus = _head_sum(r * k_h * rk_ref[...], bd) * v

    ti = lax.broadcasted_iota(jnp.int32, (tile, tile), 0)
    tj = lax.broadcasted_iota(jnp.int32, (tile, tile), 1)
    same = (ti // chunk) == (tj // chunk)
    cum = _ones_dot((same & (ti >= tj)).astype(BF16), lw)
    tot = _ones_dot(same.astype(BF16), lw)
    e_neg = jnp.exp(-cum)
    e_c = jnp.exp(tot - cum)
    kap_s[...] = kk * jnp.exp(cum - lw)
    bet_s[...] = b * e_neg
    gam_s[...] = k_h * e_neg
    rho_s[...] = r * jnp.exp(cum)
    betc_s[...] = b * e_c
    gamc_s[...] = k_h * e_c
    v_s[...] = v
    wc_s[...] = jnp.exp(tot)

    c2 = 2 * chunk
    ri = lax.broadcasted_iota(jnp.int32, (c2, c2), 0)
    rj = lax.broadcasted_iota(jnp.int32, (c2, c2), 1)
    same_head = (ri // chunk) == (rj // chunk)
    strict = same_head & (ri > rj)
    incl = same_head & (ri >= rj)
    head0 = lax.broadcasted_iota(jnp.int32, (chunk, LANES), 1) < HEAD_DIM
    n_double = int(math.log2(chunk)) - 1
    pairs = range(n_pairs)

    def stack(x):
        return jnp.concatenate([jnp.where(head0, x, 0.0), jnp.where(head0, 0.0, x)], axis=0)

    def chunk_body(ci, carry_):
        rs = pl.ds(pl.multiple_of(ci * chunk, chunk), chunk)
        cols = [slice(LANES * pr, LANES * (pr + 1)) for pr in pairs]
        bet = [bet_s[rs, ls] for ls in cols]
        gam = [gam_s[rs, ls] for ls in cols]
        s_bd = [state[pr] for pr in pairs]
        kr = [_split2(jnp.concatenate([stack(kap_s[rs, ls]), stack(rho_s[rs, ls])], axis=0)) for ls in cols]
        bet2 = [_split2(jnp.concatenate([x, x], axis=0)) for x in bet]
        gam2 = [_split2(jnp.concatenate([x, x], axis=0)) for x in gam]
        v2 = [_split2(stack(v_s[rs, ls])) for ls in cols]
        ssp = [_split2(x) for x in s_bd]
        p_b = [_dot3(kr[pr], bet2[pr], NT) for pr in pairs]
        p_g = [_dot3(kr[pr], gam2[pr], NT) for pr in pairs]
        xo0 = [_dot3(kr[pr], ssp[pr], NT) for pr in pairs]
        a1 = [_split2(jnp.where(strict, x[0:c2], 0.0)) for x in p_b]
        rb = [_split2(jnp.where(incl, x[c2:2 * c2], 0.0)) for x in p_b]
        a2 = [_split2(jnp.where(strict, x[0:c2], 0.0)) for x in p_g]
        rg = [_split2(jnp.where(incl, x[c2:2 * c2], 0.0)) for x in p_g]
        rhs = [_dot3(a2[pr], v2[pr]) for pr in pairs]
        rhs = [rhs[pr] + xo0[pr][0:c2] for pr in pairs]
        x = [rhs[pr] - _dot3(a1[pr], _split2(rhs[pr])) for pr in pairs]
        ap = a1
        for _ in range(n_double):
            ap = [_split2(_dot3(ap[pr], ap[pr])) for pr in pairs]
            x = [x[pr] + _dot3(ap[pr], _split2(x[pr])) for pr in pairs]
        u2 = [_split2(-x[pr]) for pr in pairs]
        o2 = [_dot3(rb[pr], u2[pr]) + _dot3(rg[pr], v2[pr]) for pr in pairs]
        o2 = [o2[pr] + xo0[pr][c2:2 * c2] for pr in pairs]
        for pr in pairs:
            o_s[rs, cols[pr]] = o2[pr][0:chunk] + o2[pr][chunk:c2]
        s_new = [_dot3(u2[pr], _split2(stack(betc_s[rs, cols[pr]])), TN)
                 + _dot3(v2[pr], _split2(stack(gamc_s[rs, cols[pr]])), TN) for pr in pairs]
        for pr in pairs:
            state[pr] = s_new[pr] + s_bd[pr] * wc_s[rs, cols[pr]][0:1, :]
        return carry_

    lax.fori_loop(0, tile // chunk, chunk_body, 0)

    o = o_s[...]
    mean = _head_sum(o, bd) * (1.0 / HEAD_DIM)
    oc = o - mean
    var = _head_sum(oc * oc, bd) * (1.0 / HEAD_DIM)
    on = (oc * lax.rsqrt(var + GN_EPS)) * lng_ref[...] + lnb_ref[...]
    mix_ref[0] = ((on + bonus) * _sigmoid(ga_ref[0])).astype(BF16)

    @pl.when(tb == pl.num_programs(1) - 1)
    def _():
        sout_ref[0] = state[...]


def _rwkv(p_rw, gate_a, shift0, s0_bd, mu, w0, a0, lora_up, k_k, k_a, r_k, ln_g, ln_b, *, tile, chunk):
    bsz, t, rw_proj = p_rw.shape
    width = gate_a.shape[-1]
    n_pairs = width // LANES
    tok = lambda n: pl.BlockSpec((1, tile, n), lambda b, i: (b, i, 0))
    full = lambda a: pl.BlockSpec(a.shape, lambda b, i: (0,) * a.ndim)
    per_b = lambda a: pl.BlockSpec((1,) + a.shape[1:], lambda b, i: (b,) + (0,) * (a.ndim - 1))
    scr = lambda: pltpu.VMEM((tile, width), F32)
    return pl.pallas_call(
        functools.partial(_rwkv_kernel, chunk=chunk, width=width),
        grid=(bsz, t // tile),
        in_specs=[tok(rw_proj), tok(width), per_b(shift0), per_b(s0_bd), full(mu), full(w0), full(a0),
                  full(lora_up), full(k_k), full(k_a), full(r_k), full(ln_g), full(ln_b)],
        out_specs=(tok(width), per_b(s0_bd)),
        out_shape=(jax.ShapeDtypeStruct((bsz, t, width), BF16), jax.ShapeDtypeStruct(s0_bd.shape, F32)),
        scratch_shapes=[pltpu.VMEM((1, rw_proj), F32), pltpu.VMEM((n_pairs, LANES, LANES), F32)]
                       + [scr() for _ in range(9)],
        compiler_params=pltpu.CompilerParams(dimension_semantics=("arbitrary", "arbitrary"),
                                             vmem_limit_bytes=VMEM_LIMIT),
        name="rwkv",
    )(p_rw, gate_a, shift0, s0_bd, mu, w0, a0, lora_up, k_k, k_a, r_k, ln_g, ln_b)


def _sb_logits(z, valid):
    lg = -(jnp.maximum(z, 0.0) + jnp.log(1.0 + jnp.exp(-jnp.abs(z))))
    return lg if valid is None else jnp.where(valid, lg, 0.0)


def _sb_block(z, c, tri, valid):
    hi, lo = _split2(_sb_logits(z, valid))
    incl = _dot(hi, tri) + _dot(lo, tri)
    w = jnp.exp(z + incl + c)
    if valid is not None:
        w = jnp.where(valid, w, 0.0)
    return w.astype(BF16), c + incl[:, 0:1]


def _suffix_tri(n):
    r = lax.broadcasted_iota(jnp.int32, (n, n), 0)
    c = lax.broadcasted_iota(jnp.int32, (n, n), 1)
    return (r >= c).astype(BF16)


def _sb_prompt_kernel(bias_ref, q_ref, k_ref, v_ref, gb_ref, o_ref, acc0_ref, acc1_ref, *, tq, tk):
    hp = pl.program_id(1)
    qi = pl.program_id(2)
    q = q_ref[0]
    head0 = lax.broadcasted_iota(jnp.int32, (tq, LANES), 1) < HEAD_DIM
    qh = (jnp.where(head0, q, jnp.zeros_like(q)), jnp.where(head0, jnp.zeros_like(q), q))
    bias = (bias_ref[2 * hp], bias_ref[2 * hp + 1])
    accs = (acc0_ref, acc1_ref)
    tri = _suffix_tri(tk)
    row = lax.broadcasted_iota(jnp.int32, (tq, tk), 0)
    col = lax.broadcasted_iota(jnp.int32, (tq, tk), 1)
    n_diag = tq // tk
    acc0_ref[...] = jnp.zeros_like(acc0_ref)
    acc1_ref[...] = jnp.zeros_like(acc1_ref)

    def sweep(kbs, cs, masked):
        nb = len(kbs)
        ks = [pl.multiple_of(kb * tk, tk) for kb in kbs]
        kblk = [k_ref[0, pl.ds(s, tk), :] for s in ks]
        vblk = [v_ref[0, pl.ds(s, tk), :] for s in ks]
        valid = [(col + (kb * tk - qi * tq)) < row if masked else None for kb in kbs]
        z = [[_dot(qh[h], kblk[i], NT) + bias[h] for i in range(nb)] for h in range(2)]
        sp = [[_split2(_sb_logits(z[h][i], valid[i])) for i in range(nb)] for h in range(2)]
        incl = [[_dot(sp[h][i][0], tri) + _dot(sp[h][i][1], tri) for i in range(nb)] for h in range(2)]
        new_cs = []
        for h in range(2):
            c = cs[h]
            pv = None
            for i in range(nb):
                w = jnp.exp(z[h][i] + incl[h][i] + c)
                if masked:
                    w = jnp.where(valid[i], w, 0.0)
                c = c + incl[h][i][:, 0:1]
                d = _dot(w.astype(BF16), vblk[i])
                pv = d if pv is None else pv + d
            accs[h][...] += pv
            new_cs.append(c)
        return tuple(new_cs)

    last = (qi + 1) * n_diag - 1
    zero = jnp.zeros((tq, 1), F32)
    cs = sweep([last - i for i in range(n_diag)], (zero, zero), True)
    first_diag = qi * n_diag
    lax.fori_loop(0, first_diag // 2,
                  lambda j, cs: sweep([first_diag - 1 - 2 * j, first_diag - 2 - 2 * j], cs, False), cs)
    out = jnp.where(head0, acc0_ref[...], acc1_ref[...])
    o_ref[0] = (out * _sigmoid(gb_ref[0])).astype(BF16)


def _sb_prompt(q16, k16, v16, gate_b, sb_bias, *, tq, tk):
    bsz, t, width = q16.shape
    assert (tq // tk) % 2 == 0 and t % tq == 0
    qspec = pl.BlockSpec((1, tq, LANES), lambda b, hp, i, bias: (b, i, hp))
    kspec = pl.BlockSpec((1, t, LANES), lambda b, hp, i, bias: (b, 0, hp))
    return pl.pallas_call(
        functools.partial(_sb_prompt_kernel, tq=tq, tk=tk),
        grid_spec=pltpu.PrefetchScalarGridSpec(
            num_scalar_prefetch=1,
            grid=(bsz, width // LANES, t // tq),
            in_specs=[qspec, kspec, kspec, qspec],
            out_specs=qspec,
            scratch_shapes=[pltpu.VMEM((tq, LANES), F32), pltpu.VMEM((tq, LANES), F32)],
        ),
        out_shape=jax.ShapeDtypeStruct((bsz, t, width), BF16),
        compiler_params=pltpu.CompilerParams(dimension_semantics=("arbitrary",) * 3, vmem_limit_bytes=VMEM_LIMIT),
        name="sb_prompt",
    )(sb_bias, q16, k16, v16, gate_b)


def _sb_sample_kernel(pt_ref, q_ref, bias_ref, kn_ref, vn_ref, *rest, n_q, pages_per_step, n_heads):
    kp_refs = rest[:pages_per_step]
    vp_refs = rest[pages_per_step:2 * pages_per_step]
    gb_ref, o_ref, qbd_ref, acc_ref, c_ref = rest[2 * pages_per_step:]
    j = pl.program_id(1)
    n_rows = n_heads * n_q
    width = n_heads * HEAD_DIM
    row_head = lax.broadcasted_iota(jnp.int32, (n_rows, width), 0) // n_q
    lane_head = lax.broadcasted_iota(jnp.int32, (n_rows, width), 1) // HEAD_DIM
    own = row_head == lane_head
    tri = _suffix_tri(PAGE_SIZE)

    def block(kblk, vblk, valid):
        kblk = pltpu.einshape("shd->s(hd)", kblk)
        vblk = pltpu.einshape("shd->s(hd)", vblk)
        z = _dot(qbd_ref[...], kblk.astype(BF16), NT) + bias_ref[...]
        w, c = _sb_block(z, c_ref[...], tri, valid)
        acc_ref[...] += _dot(w, vblk.astype(BF16))
        c_ref[...] = c

    @pl.when(j == 0)
    def _():
        q = q_ref[0]
        qt = jnp.concatenate([q] * n_heads, axis=0)
        qbd_ref[...] = jnp.where(own, qt, jnp.zeros_like(qt))
        acc_ref[...] = jnp.zeros_like(acc_ref)
        c_ref[...] = jnp.zeros_like(c_ref)
        tq = lax.broadcasted_iota(jnp.int32, (n_rows, PAGE_SIZE), 0) % n_q
        ks = lax.broadcasted_iota(jnp.int32, (n_rows, PAGE_SIZE), 1)
        block(kn_ref[0], vn_ref[0], ks < tq)

    @pl.when(j > 0)
    def _():
        for p in range(pages_per_step):
            block(kp_refs[p][0], vp_refs[p][0], None)

    @pl.when(j == pl.num_programs(1) - 1)
    def _():
        a = jnp.where(own, acc_ref[...], 0.0).reshape(n_heads, n_q, width)
        o_ref[0] = (jnp.sum(a, axis=0) * _sigmoid(gb_ref[0])).astype(BF16)


def _sb_sample(q16, k_new, v_new, gate_b, cache_k, cache_v, page_table, bias_rows, *, pages_per_step):
    bsz, n_q, width = q16.shape
    n_heads = width // HEAD_DIM
    n_pages = page_table.shape[1]
    n_steps = n_pages // pages_per_step

    def page_spec(p):
        def index(b, j, pt):
            step = jnp.maximum(j - 1, 0)
            return (pt[b, n_pages - 1 - (step * pages_per_step + p)], 0, 0, 0)
        return pl.BlockSpec((1, PAGE_SIZE, n_heads, HEAD_DIM), index)

    new_keys = pl.BlockSpec((1, PAGE_SIZE, n_heads, HEAD_DIM), lambda b, j, pt: (b, 0, 0, 0))
    seq = lambda n: pl.BlockSpec((1, n, width), lambda b, j, pt: (b, 0, 0))
    return pl.pallas_call(
        functools.partial(_sb_sample_kernel, n_q=n_q, pages_per_step=pages_per_step, n_heads=n_heads),
        grid_spec=pltpu.PrefetchScalarGridSpec(
            num_scalar_prefetch=1,
            grid=(bsz, n_steps + 1),
            in_specs=[seq(n_q), pl.BlockSpec(bias_rows.shape, lambda b, j, pt: (0, 0)), new_keys, new_keys]
                     + [page_spec(p) for p in range(pages_per_step)] * 2 + [seq(n_q)],
            out_specs=seq(n_q),
            scratch_shapes=[pltpu.VMEM((n_heads * n_q, width), BF16), pltpu.VMEM((n_heads * n_q, width), F32),
                            pltpu.VMEM((n_heads * n_q, 1), F32)],
        ),
        out_shape=jax.ShapeDtypeStruct((bsz, n_q, width), BF16),
        compiler_params=pltpu.CompilerParams(dimension_semantics=("arbitrary", "arbitrary"),
                                             vmem_limit_bytes=VMEM_LIMIT),
        name="sb_sample",
    )(page_table, q16, bias_rows, k_new, v_new, *([cache_k] * pages_per_step), *([cache_v] * pages_per_step),
      gate_b)


def _out_proj_kernel(x_ref, ma_ref, mb_ref, w_ref, g_ref, wr_ref, br_ref, x1_ref, h2_ref, lg_ref):
    half = ma_ref.shape[1]
    x1 = x_ref[...] + (_dot(ma_ref[...], w_ref[0:half, :]) + _dot(mb_ref[...], w_ref[half:2 * half, :]))
    x1_ref[...] = x1
    ms = jnp.mean(x1 * x1, axis=-1, keepdims=True)
    h2 = (x1 * lax.rsqrt(ms + RMS_EPS)) * g_ref[...]
    h2_ref[...] = h2.astype(BF16)
    lg_ref[...] = _dot3(_split2(h2), _split2(wr_ref[...])) + br_ref[...]


def _out_proj(x2d, mix_a, mix_b, w_out16, norm_g, w_router_pad, b_router_pad, *, tm):
    m, d = x2d.shape
    tm = min(tm, m)
    row = lambda n: pl.BlockSpec((tm, n), lambda i: (i, 0))
    full = lambda a: pl.BlockSpec(a.shape, lambda i: (0,) * a.ndim)
    n_lg = w_router_pad.shape[1]
    return pl.pallas_call(
        _out_proj_kernel,
        grid=(m // tm,),
        in_specs=[row(d), row(mix_a.shape[1]), row(mix_b.shape[1]), full(w_out16), full(norm_g),
                  full(w_router_pad), full(b_router_pad)],
        out_specs=(row(d), row(d), row(n_lg)),
        out_shape=(jax.ShapeDtypeStruct((m, d), F32), jax.ShapeDtypeStruct((m, d), BF16),
                   jax.ShapeDtypeStruct((m, n_lg), F32)),
        compiler_params=pltpu.CompilerParams(dimension_semantics=("arbitrary",), vmem_limit_bytes=VMEM_LIMIT),
        name="out_proj",
    )(x2d, mix_a, mix_b, w_out16, norm_g, w_router_pad, b_router_pad)


def _moe_kernel(te_ref, nu_ref, x_ref, gate_ref, wgu_ref, bgu_ref, wd_ref, bd_ref, o_ref):
    t = pl.program_id(0)
    d_ff = wd_ref.shape[1]

    @pl.when(t < nu_ref[0])
    def _():
        gu = _dot(x_ref[...], wgu_ref[0]) + bgu_ref[0]
        g_lin = jnp.minimum(gu[:, 0:d_ff], SWIGLU_LIMIT)
        u_lin = jnp.clip(gu[:, d_ff:2 * d_ff], -SWIGLU_LIMIT, SWIGLU_LIMIT)
        act = g_lin * _sigmoid(SWIGLU_ALPHA * g_lin) * (u_lin + 1.0)
        y = _dot(act.astype(BF16), wd_ref[0]) + bd_ref[0]
        o_ref[...] = y * gate_ref[...]

    @pl.when(t >= nu_ref[0])
    def _():
        o_ref[...] = jnp.zeros_like(o_ref)


def _moe(xs16, row_gate, tile_expert, n_used, w_gu16, b_gu, w_d16, b_d, *, tm):
    r, d = xs16.shape
    d_ff = w_d16.shape[1]
    n_tiles = r // tm
    row = lambda n: pl.BlockSpec((tm, n), lambda t, te, nu: (t, 0))
    ex = lambda a: pl.BlockSpec((1,) + a.shape[1:], lambda t, te, nu: (te[t], 0, 0))
    return pl.pallas_call(
        _moe_kernel,
        grid_spec=pltpu.PrefetchScalarGridSpec(
            num_scalar_prefetch=2,
            grid=(n_tiles,),
            in_specs=[row(d), row(1), ex(w_gu16), ex(b_gu), ex(w_d16), ex(b_d)],
            out_specs=row(d),
        ),
        out_shape=jax.ShapeDtypeStruct((r, d), F32),
        compiler_params=pltpu.CompilerParams(dimension_semantics=("arbitrary",), vmem_limit_bytes=VMEM_LIMIT),
        name="moe",
    )(tile_expert, n_used, xs16, row_gate, w_gu16, b_gu, w_d16, b_d)


def _route(logits, n_experts, tm):
    m = logits.shape[0]
    n_assign = m * TOP_K
    top_logit, top_e = lax.top_k(logits, TOP_K)
    gate = jax.nn.softmax(top_logit, axis=-1).reshape(n_assign)
    e_flat = top_e.reshape(n_assign).astype(jnp.int32)
    onehot = (e_flat[:, None] == jnp.arange(n_experts, dtype=jnp.int32)[None, :]).astype(jnp.int32)
    csum = jnp.cumsum(onehot, axis=0)
    rank = jnp.sum((csum - onehot) * onehot, axis=1)
    counts = csum[-1]
    padded = (counts + tm - 1) // tm * tm
    padded_end = jnp.cumsum(padded)
    dest = (padded_end - padded)[e_flat] + rank
    n_tiles = (n_assign + n_experts * (tm - 1)) // tm + 1
    n_rows = n_tiles * tm
    src = jnp.zeros((n_rows,), jnp.int32).at[dest].set(jnp.arange(1, n_assign + 1, dtype=jnp.int32))
    filled = src > 0
    src = jnp.maximum(src - 1, 0)
    row_tok = jnp.where(filled, src // TOP_K, 0)
    row_gate = jnp.where(filled, gate[src], 0.0)
    tile_expert = jnp.minimum(
        jnp.searchsorted(padded_end, jnp.arange(n_tiles, dtype=jnp.int32) * tm, side='right'),
        n_experts - 1).astype(jnp.int32)
    n_used = (padded_end[-1] // tm).astype(jnp.int32).reshape(1)
    return row_tok, row_gate[:, None], tile_expert, n_used, dest


def _pair_block_diag(s):
    bsz, h = s.shape[:2]
    s = s.reshape(bsz, h // 2, 2, HEAD_DIM, HEAD_DIM)
    eye = jnp.eye(2, dtype=s.dtype)
    return jnp.einsum('bpivk,ij->bpivjk', s, eye).reshape(bsz, h // 2, LANES, LANES)


def _pair_diag_blocks(s_bd):
    bsz, n_pairs = s_bd.shape[:2]
    s = s_bd.reshape(bsz, n_pairs, 2, HEAD_DIM, 2, HEAD_DIM)
    return jnp.stack([s[:, :, 0, :, 0, :], s[:, :, 1, :, 1, :]], axis=2).reshape(
        bsz, n_pairs * 2, HEAD_DIM, HEAD_DIM)


def kernel(x_prompt, x_sample, cache_k, cache_v, page_table, state_wkv, state_shift, norm1_g, w_in, mu_shift, w0, w_lora_up, a0, a_lora_up, k_k, k_a, r_k, ln_x_g, ln_x_b, q_norm_g, k_norm_g, sb_bias, w_out, norm2_g, w_router, b_router, w_gate_up, b_gate_up, w_down, b_down):
    d_model = x_prompt.shape[-1]
    rw_proj = mu_shift.shape[0]
    width = w0.shape[0]
    n_heads = width // HEAD_DIM
    n_experts = w_router.shape[1]
    row = lambda a: a.reshape(1, -1).astype(F32)

    w_in16 = w_in.astype(BF16)
    w_out16 = w_out.astype(BF16)
    w_gu16 = w_gate_up.astype(BF16)
    w_d16 = w_down.astype(BF16)
    qg = row(jnp.tile(q_norm_g, n_heads))
    kg = row(jnp.tile(k_norm_g, n_heads))
    lora_up = jnp.zeros((W_RANK + A_RANK, 2 * width), F32)
    lora_up = lora_up.at[:W_RANK, :width].set(w_lora_up).at[W_RANK:, width:].set(a_lora_up)
    w_router_pad = jnp.zeros((d_model, LANES), F32).at[:, :n_experts].set(w_router)
    b_router_pad = jnp.zeros((1, LANES), F32).at[0, :n_experts].set(b_router)
    rwkv_params = (row(mu_shift), row(w0), row(a0), lora_up, row(k_k), row(k_a), row(r_k), row(ln_x_g), row(ln_x_b))

    def front(x, shift0, s0, tile, chunk):
        bsz, t, _ = x.shape
        x2d = x.reshape(bsz * t, d_model)
        p_rw, q16, kb, k16, vb, v16, gate_a, gate_b = _in_proj(
            x2d, row(norm1_g), w_in16, qg, kg, rw_proj=rw_proj, width=width, tm=256)
        seq = lambda a: a.reshape(bsz, t, a.shape[-1])
        p_rw = seq(p_rw)
        mix_a, s_bd = _rwkv(p_rw, seq(gate_a), shift0[:, None, :], _pair_block_diag(s0), *rwkv_params,
                            tile=tile, chunk=chunk)
        return x2d, p_rw, seq(q16), kb, seq(k16), vb, seq(v16), seq(gate_b), mix_a, _pair_diag_blocks(s_bd)

    bp, tp, _ = x_prompt.shape
    bs, ts, _ = x_sample.shape
    xp2d, prw_p, q_p, kb_p, k16_p, vb_p, v16_p, gb_p, mixa_p, wkv_p = front(
        x_prompt, jnp.zeros((bp, rw_proj), F32), jnp.zeros((bp, n_heads, HEAD_DIM, HEAD_DIM), F32), 256, 64)
    xs2d, prw_s, q_s, kb_s, k16_s, vb_s, v16_s, gb_s, mixa_s, wkv_s = front(
        x_sample, state_shift, state_wkv, ts, ts)

    mixb_p = _sb_prompt(q_p, k16_p, v16_p, gb_p, sb_bias.astype(F32), tq=512, tk=128)

    hd = (n_heads, HEAD_DIM)
    pad_keys = lambda a: jnp.pad(a.reshape(bs, ts, *hd), ((0, 0), (0, PAGE_SIZE - ts), (0, 0), (0, 0)))
    mixb_s = _sb_sample(q_s, pad_keys(kb_s), pad_keys(vb_s), gb_s, cache_k, cache_v,
                        page_table.astype(jnp.int32), jnp.repeat(sb_bias.astype(F32), ts)[:, None],
                        pages_per_step=4)

    def back(x2d, mix_a, mix_b):
        m = x2d.shape[0]
        return _out_proj(x2d, mix_a.reshape(m, width), mix_b.reshape(m, width), w_out16, row(norm2_g),
                         w_router_pad, b_router_pad, tm=256)

    x1_p, h2_p, lg_p = back(xp2d, mixa_p, mixb_p)
    x1_s, h2_s, lg_s = back(xs2d, mixa_s, mixb_s)

    x1 = jnp.concatenate([x1_p, x1_s], axis=0)
    h2 = jnp.concatenate([h2_p, h2_s], axis=0)
    logits = jnp.concatenate([lg_p, lg_s], axis=0)[:, :n_experts]
    m_all = x1.shape[0]
    tm_moe = 512
    row_tok, row_gate, tile_expert, n_used, dest = _route(logits, n_experts, tm_moe)
    y_rows = _moe(h2[row_tok], row_gate, tile_expert, n_used, w_gu16, b_gate_up[:, None, :].astype(F32),
                  w_d16, b_down[:, None, :].astype(F32), tm=tm_moe)
    y = x1 + jnp.sum(y_rows[dest.reshape(m_all, TOP_K)], axis=1)

    mp = bp * tp
    return (y[:mp].reshape(bp, tp, d_model), y[mp:].reshape(bs, ts, d_model),
            kb_p.reshape(bp, tp, *hd), vb_p.reshape(bp, tp, *hd),
            kb_s.reshape(bs, ts, *hd), vb_s.reshape(bs, ts, *hd),
            wkv_p, wkv_s, prw_p[:, -1], prw_s[:, -1])
```

```python
import functools
import math

import jax
import jax.numpy as jnp
from jax import lax
from jax.experimental import pallas as pl
from jax.experimental.pallas import tpu as pltpu

F32 = jnp.float32
BF16 = jnp.bfloat16

HEAD_DIM = 64
LANES = 128
PAGE_SIZE = 128
W_RANK = 64
A_RANK = 64
DECAY_SCALE = math.exp(-0.5)
GN_EPS = 64e-5
RMS_EPS = 1e-6
SB_SCALE = HEAD_DIM ** -0.5
TOP_K = 4
SWIGLU_ALPHA = 1.702
SWIGLU_LIMIT = 7.0
VMEM_LIMIT = 56 * 1024 * 1024

NN = (((1,), (0,)), ((), ()))
NT = (((1,), (1,)), ((), ()))
TN = (((0,), (0,)), ((), ()))


def _dot(a, b, dims=NN):
    return lax.dot_general(a, b, dims, preferred_element_type=F32)


def _split2(x):
    hi = x.astype(BF16)
    lo = (x - hi.astype(F32)).astype(BF16)
    return hi, lo


def _split3(x):
    hi = x.astype(BF16)
    r1 = x - hi.astype(F32)
    mid = r1.astype(BF16)
    lo = (r1 - mid.astype(F32)).astype(BF16)
    return hi, mid, lo


def _dot3(a, b, dims=NN):
    (ah, al), (bh, bl) = a, b
    return _dot(ah, bh, dims) + (_dot(ah, bl, dims) + _dot(al, bh, dims))


def _ones_dot(m, x):
    hi, mid, lo = _split3(x)
    return _dot(m, hi) + (_dot(m, mid) + _dot(m, lo))


def _pair_ones():
    r = lax.broadcasted_iota(jnp.int32, (LANES, LANES), 0) // HEAD_DIM
    c = lax.broadcasted_iota(jnp.int32, (LANES, LANES), 1) // HEAD_DIM
    return (r == c).astype(BF16)


def _head_sum(x, bd):
    hi, lo = _split2(x)
    parts = []
    for p in range(x.shape[1] // LANES):
        sl = slice(LANES * p, LANES * (p + 1))
        parts.append(_dot(hi[:, sl], bd) + _dot(lo[:, sl], bd))
    return parts[0] if len(parts) == 1 else jnp.concatenate(parts, axis=1)


def _sigmoid(x):
    return 1.0 / (1.0 + jnp.exp(-x))


def _in_proj_kernel(x_ref, g_ref, w_ref, qg_ref, kg_ref,
                    prw_ref, q16_ref, kb_ref, k16_ref, vb_ref, v16_ref, ga_ref, gb_ref, *, rw_proj, width):
    x = x_ref[...]
    ms = jnp.mean(x * x, axis=-1, keepdims=True)
    h = ((x * lax.rsqrt(ms + RMS_EPS)) * g_ref[...]).astype(BF16)
    bd = _pair_ones()
    o = rw_proj
    prw_ref[...] = _dot(h, w_ref[:, 0:o])
    q = _dot(h, w_ref[:, o:o + width])
    qn = (q * lax.rsqrt(_head_sum(q * q, bd) * (1.0 / HEAD_DIM) + RMS_EPS)) * qg_ref[...]
    q16_ref[...] = (qn * SB_SCALE).astype(BF16)
    k = _dot(h, w_ref[:, o + width:o + 2 * width])
    kn = (k * lax.rsqrt(_head_sum(k * k, bd) * (1.0 / HEAD_DIM) + RMS_EPS)) * kg_ref[...]
    kb_ref[...] = kn
    k16_ref[...] = kn.astype(BF16)
    v = _dot(h, w_ref[:, o + 2 * width:o + 3 * width])
    vb_ref[...] = v
    v16_ref[...] = v.astype(BF16)
    ga_ref[...] = _dot(h, w_ref[:, o + 3 * width:o + 4 * width])
    gb_ref[...] = _dot(h, w_ref[:, o + 4 * width:o + 5 * width])


def _in_proj(x2d, norm_g, w_in16, qg, kg, *, rw_proj, width, tm):
    m, d = x2d.shape
    tm = min(tm, m)
    row = lambda n: pl.BlockSpec((tm, n), lambda i: (i, 0))
    full = lambda a: pl.BlockSpec(a.shape, lambda i: (0,) * a.ndim)
    out_shapes = (
        jax.ShapeDtypeStruct((m, rw_proj), F32),
        jax.ShapeDtypeStruct((m, width), BF16),
        jax.ShapeDtypeStruct((m, width), F32), jax.ShapeDtypeStruct((m, width), BF16),
        jax.ShapeDtypeStruct((m, width), F32), jax.ShapeDtypeStruct((m, width), BF16),
        jax.ShapeDtypeStruct((m, width), F32), jax.ShapeDtypeStruct((m, width), F32),
    )
    return pl.pallas_call(
        functools.partial(_in_proj_kernel, rw_proj=rw_proj, width=width),
        grid=(m // tm,),
        in_specs=[row(d), full(norm_g), full(w_in16), full(qg), full(kg)],
        out_specs=tuple(row(s.shape[1]) for s in out_shapes),
        out_shape=out_shapes,
        compiler_params=pltpu.CompilerParams(dimension_semantics=("arbitrary",), vmem_limit_bytes=VMEM_LIMIT),
        name="in_proj",
    )(x2d, norm_g, w_in16, qg, kg)


def _rwkv_kernel(p_ref, ga_ref, sh0_ref, s0_ref, mu_ref, w0_ref, a0_ref, lora_ref, kk_ref, ka_ref, rk_ref,
                 lng_ref, lnb_ref, mix_ref, sout_ref,
                 carry, state, kap_s, bet_s, gam_s, rho_s, betc_s, gamc_s, v_s, wc_s, o_s, *, chunk, width):
    tb = pl.program_id(1)
    tile = p_ref.shape[1]
    n_pairs = width // LANES

    @pl.when(tb == 0)
    def _():
        carry[...] = sh0_ref[0]
        state[...] = s0_ref[0]

    p = p_ref[0]
    rows = lax.broadcasted_iota(jnp.int32, p.shape, 0)
    prev = jnp.where(rows == 0, carry[...], pltpu.roll(p, 1, axis=0))
    carry[...] = p[tile - 1:tile, :]
    xm = p + (prev - p) * mu_ref[...]
    r = xm[:, 0:width]
    k = xm[:, width:2 * width]
    v = xm[:, 2 * width:3 * width]
    wa = xm[:, 3 * width:3 * width + W_RANK + A_RANK]
    lane = lax.broadcasted_iota(jnp.int32, wa.shape, 1)
    lora_in = jnp.where(lane < W_RANK, jnp.tanh(wa), wa)
    lora = _dot3(_split2(lora_in), _split2(lora_ref[...]))
    lw = -DECAY_SCALE * _sigmoid(w0_ref[...] + lora[:, 0:width])
    a = _sigmoid(a0_ref[...] + lora[:, width:2 * width])
    bd = _pair_ones()
    kk = k * kk_ref[...]
    kk = kk / jnp.maximum(jnp.sqrt(_head_sum(kk * kk, bd)), 1e-12)
    k_h = k * (1.0 + (a - 1.0) * ka_ref[...])
    b = kk * a
    bonus = _head_sum(r * k_h * rk_ref[...], bd) * v

    ti = lax.broadcasted_iota(jnp.int32, (tile, tile), 0)
    tj = lax.broadcasted_iota(jnp.int32, (tile, tile), 1)
    same = (ti // chunk) == (tj // chunk)
    cum = _ones_dot((same & (ti >= tj)).astype(BF16), lw)
    tot = _ones_dot(same.astype(BF16), lw)
    e_neg = jnp.exp(-cum)
    e_c = jnp.exp(tot - cum)
    kap_s[...] = kk * jnp.exp(cum - lw)
    bet_s[...] = b * e_neg
    gam_s[...] = k_h * e_neg
    rho_s[...] = r * jnp.exp(cum)
    betc_s[...] = b * e_c
    gamc_s[...] = k_h * e_c
    v_s[...] = v
    wc_s[...] = jnp.exp(tot)

    c2 = 2 * chunk
    ri = lax.broadcasted_iota(jnp.int32, (c2, c2), 0)
    rj = lax.broadcasted_iota(jnp.int32, (c2, c2), 1)
    same_head = (ri // chunk) == (rj // chunk)
    strict = same_head & (ri > rj)
    incl = same_head & (ri >= rj)
    head0 = lax.broadcasted_iota(jnp.int32, (chunk, LANES), 1) < HEAD_DIM
    n_double = int(math.log2(chunk)) - 1
    pairs = range(n_pairs)

    def stack(x):
        return jnp.concatenate([jnp.where(head0, x, 0.0), jnp.where(head0, 0.0, x)], axis=0)

    def chunk_body(ci, carry_):
        rs = pl.ds(pl.multiple_of(ci * chunk, chunk), chunk)
        cols = [slice(LANES * pr, LANES * (pr + 1)) for pr in pairs]
        bet = [bet_s[rs, ls] for ls in cols]
        gam = [gam_s[rs, ls] for ls in cols]
        s_bd = [state[pr] for pr in pairs]
        kr = [_split2(jnp.concatenate([stack(kap_s[rs, ls]), stack(rho_s[rs, ls])], axis=0)) for ls in cols]
        bet2 = [_split2(jnp.concatenate([x, x], axis=0)) for x in bet]
        gam2 = [_split2(jnp.concatenate([x, x], axis=0)) for x in gam]
        v2 = [_split2(stack(v_s[rs, ls])) for ls in cols]
        ssp = [_split2(x) for x in s_bd]
        p_b = [_dot3(kr[pr], bet2[pr], NT) for pr in pairs]
        p_g = [_dot3(kr[pr], gam2[pr], NT) for pr in pairs]
        xo0 = [_dot3(kr[pr], ssp[pr], NT) for pr in pairs]
        a1 = [_split2(jnp.where(strict, x[0:c2], 0.0)) for x in p_b]
        rb = [_split2(jnp.where(incl, x[c2:2 * c2], 0.0)) for x in p_b]
        a2 = [_split2(jnp.where(strict, x[0:c2], 0.0)) for x in p_g]
        rg = [_split2(jnp.where(incl, x[c2:2 * c2], 0.0)) for x in p_g]
        rhs = [_dot3(a2[pr], v2[pr]) for pr in pairs]
        rhs = [rhs[pr] + xo0[pr][0:c2] for pr in pairs]
        x = [rhs[pr] - _dot3(a1[pr], _split2(rhs[pr])) for pr in pairs]
        ap = a1
        for _ in range(n_double):
            ap = [_split2(_dot3(ap[pr], ap[pr])) for pr in pairs]
            x = [x[pr] + _dot3(ap[pr], _split2(x[pr])) for pr in pairs]
        u2 = [_split2(-x[pr]) for pr in pairs]
        o2 = [_dot3(rb[pr], u2[pr]) + _dot3(rg[pr], v2[pr]) for pr in pairs]
        o2 = [o2[pr] + xo0[pr][c2:2 * c2] for pr in pairs]
        for pr in pairs:
            o_s[rs, cols[pr]] = o2[pr][0:chunk] + o2[pr][chunk:c2]
        s_new = [_dot3(u2[pr], _split2(stack(betc_s[rs, cols[pr]])), TN)
                 + _dot3(v2[pr], _split2(stack(gamc_s[rs, cols[pr]])), TN) for pr in pairs]
        for pr in pairs:
            state[pr] = s_new[pr] + s_bd[pr] * wc_s[rs, cols[pr]][0:1, :]
        return carry_

    lax.fori_loop(0, tile // chunk, chunk_body, 0)

    o = o_s[...]
    mean = _head_sum(o, bd) * (1.0 / HEAD_DIM)
    oc = o - mean
    var = _head_sum(oc * oc, bd) * (1.0 / HEAD_DIM)
    on = (oc * lax.rsqrt(var + GN_EPS)) * lng_ref[...] + lnb_ref[...]
    mix_ref[0] = ((on + bonus) * _sigmoid(ga_ref[0])).astype(BF16)

    @pl.when(tb == pl.num_programs(1) - 1)
    def _():
        sout_ref[0] = state[...]


def _rwkv(p_rw, gate_a, shift0, s0_bd, mu, w0, a0, lora_up, k_k, k_a, r_k, ln_g, ln_b, *, tile, chunk):
    bsz, t, rw_proj = p_rw.shape
    width = gate_a.shape[-1]
    n_pairs = width // LANES
    tok = lambda n: pl.BlockSpec((1, tile, n), lambda b, i: (b, i, 0))
    full = lambda a: pl.BlockSpec(a.shape, lambda b, i: (0,) * a.ndim)
    per_b = lambda a: pl.BlockSpec((1,) + a.shape[1:], lambda b, i: (b,) + (0,) * (a.ndim - 1))
    scr = lambda: pltpu.VMEM((tile, width), F32)
    return pl.pallas_call(
        functools.partial(_rwkv_kernel, chunk=chunk, width=width),
        grid=(bsz, t // tile),
        in_specs=[tok(rw_proj), tok(width), per_b(shift0), per_b(s0_bd), full(mu), full(w0), full(a0),
                  full(lora_up), full(k_k), full(k_a), full(r_k), full(ln_g), full(ln_b)],
        out_specs=(tok(width), per_b(s0_bd)),
        out_shape=(jax.ShapeDtypeStruct((bsz, t, width), BF16), jax.ShapeDtypeStruct(s0_bd.shape, F32)),
        scratch_shapes=[pltpu.VMEM((1, rw_proj), F32), pltpu.VMEM((n_pairs, LANES, LANES), F32)]
                       + [scr() for _ in range(9)],
        compiler_params=pltpu.CompilerParams(dimension_semantics=("arbitrary", "arbitrary"),
                                             vmem_limit_bytes=VMEM_LIMIT),
        name="rwkv",
    )(p_rw, gate_a, shift0, s0_bd, mu, w0, a0, lora_up, k_k, k_a, r_k, ln_g, ln_b)


def _sb_logits(z, valid):
    lg = -(jnp.maximum(z, 0.0) + jnp.log(1.0 + jnp.exp(-jnp.abs(z))))
    return lg if valid is None else jnp.where(valid, lg, 0.0)


def _suffix_tri(n):
    r = lax.broadcasted_iota(jnp.int32, (n, n), 0)
    c = lax.broadcasted_iota(jnp.int32, (n, n), 1)
    return (r >= c).astype(BF16)


def _sb_prompt_kernel(bias_ref, q_ref, k_ref, v_ref, gb_ref, o_ref, acc0_ref, acc1_ref, *, tq, tk):
    hp = pl.program_id(1)
    qi = pl.program_id(2)
    q = q_ref[0]
    head0 = lax.broadcasted_iota(jnp.int32, (tq, LANES), 1) < HEAD_DIM
    qh = (jnp.where(head0, q, jnp.zeros_like(q)), jnp.where(head0, jnp.zeros_like(q), q))
    bias = (bias_ref[2 * hp], bias_ref[2 * hp + 1])
    accs = (acc0_ref, acc1_ref)
    tri = _suffix_tri(tk)
    row = lax.broadcasted_iota(jnp.int32, (tq, tk), 0)
    col = lax.broadcasted_iota(jnp.int32, (tq, tk), 1)
    n_diag = tq // tk
    acc0_ref[...] = jnp.zeros_like(acc0_ref)
    acc1_ref[...] = jnp.zeros_like(acc1_ref)

    def sweep(kbs, cs, masked):
        nb = len(kbs)
        ks = [pl.multiple_of(kb * tk, tk) for kb in kbs]
        kblk = [k_ref[0, pl.ds(s, tk), :] for s in ks]
        vblk = [v_ref[0, pl.ds(s, tk), :] for s in ks]
        valid = [(col + (kb * tk - qi * tq)) < row if masked else None for kb in kbs]
        z = [[_dot(qh[h], kblk[i], NT) + bias[h] for i in range(nb)] for h in range(2)]
        sp = [[_split2(_sb_logits(z[h][i], valid[i])) for i in range(nb)] for h in range(2)]
        incl = [[_dot(sp[h][i][0], tri) + _dot(sp[h][i][1], tri) for i in range(nb)] for h in range(2)]
        new_cs = []
        for h in range(2):
            c = cs[h]
            pv = None
            for i in range(nb):
                w = jnp.exp(z[h][i] + incl[h][i] + c)
                if masked:
                    w = jnp.where(valid[i], w, 0.0)
                c = c + incl[h][i][:, 0:1]
                d = _dot(w.astype(BF16), vblk[i])
                pv = d if pv is None else pv + d
            accs[h][...] += pv
            new_cs.append(c)
        return tuple(new_cs)

    last = (qi + 1) * n_diag - 1
    zero = jnp.zeros((tq, 1), F32)
    cs = sweep([last - i for i in range(n_diag)], (zero, zero), True)
    first_diag = qi * n_diag
    lax.fori_loop(0, first_diag // 2,
                  lambda j, cs: sweep([first_diag - 1 - 2 * j, first_diag - 2 - 2 * j], cs, False), cs)
    out = jnp.where(head0, acc0_ref[...], acc1_ref[...])
    o_ref[0] = (out * _sigmoid(gb_ref[0])).astype(BF16)


def _sb_prompt(q16, k16, v16, gate_b, sb_bias, *, tq, tk):
    bsz, t, width = q16.shape
    assert (tq // tk) % 2 == 0 and t % tq == 0
    qspec = pl.BlockSpec((1, tq, LANES), lambda b, hp, i, bias: (b, i, hp))
    kspec = pl.BlockSpec((1, t, LANES), lambda b, hp, i, bias: (b, 0, hp))
    return pl.pallas_call(
        functools.partial(_sb_prompt_kernel, tq=tq, tk=tk),
        grid_spec=pltpu.PrefetchScalarGridSpec(
            num_scalar_prefetch=1,
            grid=(bsz, width // LANES, t // tq),
            in_specs=[qspec, kspec, kspec, qspec],
            out_specs=qspec,
            scratch_shapes=[pltpu.VMEM((tq, LANES), F32), pltpu.VMEM((tq, LANES), F32)],
        ),
        out_shape=jax.ShapeDtypeStruct((bsz, t, width), BF16),
        compiler_params=pltpu.CompilerParams(dimension_semantics=("arbitrary",) * 3, vmem_limit_bytes=VMEM_LIMIT),
        name="sb_prompt",
    )(sb_bias, q16, k16, v16, gate_b)


def _sb_sample_kernel(pt_ref, q_ref, bias_ref, kn_ref, vn_ref, *rest, n_q, pages_per_step, n_heads):
    kp_refs = rest[:pages_per_step]
    vp_refs = rest[pages_per_step:2 * pages_per_step]
    gb_ref, o_ref, qbd_ref, acc_ref, c_ref = rest[2 * pages_per_step:]
    j = pl.program_id(1)
    n_rows = n_heads * n_q
    width = n_heads * HEAD_DIM
    row_head = lax.broadcasted_iota(jnp.int32, (n_rows, width), 0) // n_q
    lane_head = lax.broadcasted_iota(jnp.int32, (n_rows, width), 1) // HEAD_DIM
    own = row_head == lane_head
    tri = _suffix_tri(PAGE_SIZE)

    def blocks(kt_refs, vt_refs, valid):
        z = [_dot(qbd_ref[...], kt[0].astype(BF16)) + bias_ref[...] for kt in kt_refs]
        sp = [_split2(_sb_logits(zi, valid)) for zi in z]
        incl = [_dot(hi, tri) + _dot(lo, tri) for hi, lo in sp]
        c = c_ref[...]
        pv = None
        for i, vt in enumerate(vt_refs):
            w = jnp.exp(z[i] + incl[i] + c)
            if valid is not None:
                w = jnp.where(valid, w, 0.0)
            c = c + incl[i][:, 0:1]
            d = _dot(w.astype(BF16), vt[0].astype(BF16), NT)
            pv = d if pv is None else pv + d
        acc_ref[...] += pv
        c_ref[...] = c

    @pl.when(j == 0)
    def _():
        q = q_ref[0]
        qt = jnp.concatenate([q] * n_heads, axis=0)
        qbd_ref[...] = jnp.where(own, qt, jnp.zeros_like(qt))
        acc_ref[...] = jnp.zeros_like(acc_ref)
        c_ref[...] = jnp.zeros_like(c_ref)
        tq = lax.broadcasted_iota(jnp.int32, (n_rows, PAGE_SIZE), 0) % n_q
        ks = lax.broadcasted_iota(jnp.int32, (n_rows, PAGE_SIZE), 1)
        blocks([kn_ref], [vn_ref], ks < tq)

    @pl.when(j > 0)
    def _():
        blocks(kp_refs, vp_refs, None)

    @pl.when(j == pl.num_programs(1) - 1)
    def _():
        a = jnp.where(own, acc_ref[...], 0.0).reshape(n_heads, n_q, width)
        o_ref[0] = (jnp.sum(a, axis=0) * _sigmoid(gb_ref[0])).astype(BF16)


def _sb_sample(q16, k_new, v_new, gate_b, cache_k, cache_v, page_table, bias_rows, *, pages_per_step):
    bsz, n_q, width = q16.shape
    n_heads = width // HEAD_DIM
    n_pages = page_table.shape[1]
    n_steps = n_pages // pages_per_step

    def page_spec(p):
        def index(b, j, pt):
            step = jnp.maximum(j - 1, 0)
            return (pt[b, n_pages - 1 - (step * pages_per_step + p)], 0, 0)
        return pl.BlockSpec((1, width, PAGE_SIZE), index)

    new_keys = pl.BlockSpec((1, width, PAGE_SIZE), lambda b, j, pt: (b, 0, 0))
    seq = lambda n: pl.BlockSpec((1, n, width), lambda b, j, pt: (b, 0, 0))
    return pl.pallas_call(
        functools.partial(_sb_sample_kernel, n_q=n_q, pages_per_step=pages_per_step, n_heads=n_heads),
        grid_spec=pltpu.PrefetchScalarGridSpec(
            num_scalar_prefetch=1,
            grid=(bsz, n_steps + 1),
            in_specs=[seq(n_q), pl.BlockSpec(bias_rows.shape, lambda b, j, pt: (0, 0)), new_keys, new_keys]
                     + [page_spec(p) for p in range(pages_per_step)] * 2 + [seq(n_q)],
            out_specs=seq(n_q),
            scratch_shapes=[pltpu.VMEM((n_heads * n_q, width), BF16), pltpu.VMEM((n_heads * n_q, width), F32),
                            pltpu.VMEM((n_heads * n_q, 1), F32)],
        ),
        out_shape=jax.ShapeDtypeStruct((bsz, n_q, width), BF16),
        compiler_params=pltpu.CompilerParams(dimension_semantics=("arbitrary", "arbitrary"),
                                             vmem_limit_bytes=VMEM_LIMIT),
        name="sb_sample",
    )(page_table, q16, bias_rows, k_new, v_new, *([cache_k] * pages_per_step), *([cache_v] * pages_per_step),
      gate_b)


def _out_proj_kernel(x_ref, ma_ref, mb_ref, w_ref, g_ref, wr_ref, br_ref, x1_ref, h2_ref, lg_ref):
    half = ma_ref.shape[1]
    x1 = x_ref[...] + (_dot(ma_ref[...], w_ref[0:half, :]) + _dot(mb_ref[...], w_ref[half:2 * half, :]))
    x1_ref[...] = x1
    ms = jnp.mean(x1 * x1, axis=-1, keepdims=True)
    h2 = (x1 * lax.rsqrt(ms + RMS_EPS)) * g_ref[...]
    h2_ref[...] = h2.astype(BF16)
    lg_ref[...] = _dot3(_split2(h2), _split2(wr_ref[...])) + br_ref[...]


def _out_proj(x2d, mix_a, mix_b, w_out16, norm_g, w_router_pad, b_router_pad, *, tm):
    m, d = x2d.shape
    tm = min(tm, m)
    row = lambda n: pl.BlockSpec((tm, n), lambda i: (i, 0))
    full = lambda a: pl.BlockSpec(a.shape, lambda i: (0,) * a.ndim)
    n_lg = w_router_pad.shape[1]
    return pl.pallas_call(
        _out_proj_kernel,
        grid=(m // tm,),
        in_specs=[row(d), row(mix_a.shape[1]), row(mix_b.shape[1]), full(w_out16), full(norm_g),
                  full(w_router_pad), full(b_router_pad)],
        out_specs=(row(d), row(d), row(n_lg)),
        out_shape=(jax.ShapeDtypeStruct((m, d), F32), jax.ShapeDtypeStruct((m, d), BF16),
                   jax.ShapeDtypeStruct((m, n_lg), F32)),
        compiler_params=pltpu.CompilerParams(dimension_semantics=("arbitrary",), vmem_limit_bytes=VMEM_LIMIT),
        name="out_proj",
    )(x2d, mix_a, mix_b, w_out16, norm_g, w_router_pad, b_router_pad)


def _moe_kernel(te_ref, nu_ref, x_ref, gate_ref, wgu_ref, bgu_ref, wd_ref, bd_ref, o_ref):
    t = pl.program_id(0)
    d_ff = wd_ref.shape[1]

    @pl.when(t < nu_ref[0])
    def _():
        gu = _dot(x_ref[...], wgu_ref[0]) + bgu_ref[0]
        g_lin = jnp.minimum(gu[:, 0:d_ff], SWIGLU_LIMIT)
        u_lin = jnp.clip(gu[:, d_ff:2 * d_ff], -SWIGLU_LIMIT, SWIGLU_LIMIT)
        act = g_lin * _sigmoid(SWIGLU_ALPHA * g_lin) * (u_lin + 1.0)
        y = _dot(act.astype(BF16), wd_ref[0]) + bd_ref[0]
        o_ref[...] = y * gate_ref[...]

    @pl.when(t >= nu_ref[0])
    def _():
        o_ref[...] = jnp.zeros_like(o_ref)


def _moe(xs16, row_gate, tile_expert, n_used, w_gu16, b_gu, w_d16, b_d, *, tm):
    r, d = xs16.shape
    d_ff = w_d16.shape[1]
    n_tiles = r // tm
    row = lambda n: pl.BlockSpec((tm, n), lambda t, te, nu: (t, 0))
    ex = lambda a: pl.BlockSpec((1,) + a.shape[1:], lambda t, te, nu: (te[t], 0, 0))
    return pl.pallas_call(
        _moe_kernel,
        grid_spec=pltpu.PrefetchScalarGridSpec(
            num_scalar_prefetch=2,
            grid=(n_tiles,),
            in_specs=[row(d), row(1), ex(w_gu16), ex(b_gu), ex(w_d16), ex(b_d)],
            out_specs=row(d),
        ),
        out_shape=jax.ShapeDtypeStruct((r, d), F32),
        compiler_params=pltpu.CompilerParams(dimension_semantics=("arbitrary",), vmem_limit_bytes=VMEM_LIMIT),
        name="moe",
    )(tile_expert, n_used, xs16, row_gate, w_gu16, b_gu, w_d16, b_d)


def _route(logits, n_experts, tm):
    m = logits.shape[0]
    n_assign = m * TOP_K
    top_logit, top_e = lax.top_k(logits, TOP_K)
    gate = jax.nn.softmax(top_logit, axis=-1).reshape(n_assign)
    e_flat = top_e.reshape(n_assign).astype(jnp.int32)
    onehot = (e_flat[:, None] == jnp.arange(n_experts, dtype=jnp.int32)[None, :]).astype(jnp.int32)
    csum = jnp.cumsum(onehot, axis=0)
    rank = jnp.sum((csum - onehot) * onehot, axis=1)
    counts = csum[-1]
    padded = (counts + tm - 1) // tm * tm
    padded_end = jnp.cumsum(padded)
    dest = (padded_end - padded)[e_flat] + rank
    n_tiles = (n_assign + n_experts * (tm - 1)) // tm + 1
    n_rows = n_tiles * tm
    src = jnp.zeros((n_rows,), jnp.int32).at[dest].set(jnp.arange(1, n_assign + 1, dtype=jnp.int32))
    filled = src > 0
    src = jnp.maximum(src - 1, 0)
    row_tok = jnp.where(filled, src // TOP_K, 0)
    row_gate = jnp.where(filled, gate[src], 0.0)
    tile_expert = jnp.minimum(
        jnp.searchsorted(padded_end, jnp.arange(n_tiles, dtype=jnp.int32) * tm, side='right'),
        n_experts - 1).astype(jnp.int32)
    n_used = (padded_end[-1] // tm).astype(jnp.int32).reshape(1)
    return row_tok, row_gate[:, None], tile_expert, n_used, dest


def _pair_block_diag(s):
    bsz, h = s.shape[:2]
    s = s.reshape(bsz, h // 2, 2, HEAD_DIM, HEAD_DIM)
    eye = jnp.eye(2, dtype=s.dtype)
    return jnp.einsum('bpivk,ij->bpivjk', s, eye).reshape(bsz, h // 2, LANES, LANES)


def _pair_diag_blocks(s_bd):
    bsz, n_pairs = s_bd.shape[:2]
    s = s_bd.reshape(bsz, n_pairs, 2, HEAD_DIM, 2, HEAD_DIM)
    return jnp.stack([s[:, :, 0, :, 0, :], s[:, :, 1, :, 1, :]], axis=2).reshape(
        bsz, n_pairs * 2, HEAD_DIM, HEAD_DIM)


def kernel(x_prompt, x_sample, cache_k, cache_v, page_table, state_wkv, state_shift, norm1_g, w_in, mu_shift, w0, w_lora_up, a0, a_lora_up, k_k, k_a, r_k, ln_x_g, ln_x_b, q_norm_g, k_norm_g, sb_bias, w_out, norm2_g, w_router, b_router, w_gate_up, b_gate_up, w_down, b_down):
    d_model = x_prompt.shape[-1]
    rw_proj = mu_shift.shape[0]
    width = w0.shape[0]
    n_heads = width // HEAD_DIM
    n_experts = w_router.shape[1]
    row = lambda a: a.reshape(1, -1).astype(F32)

    w_in16 = w_in.astype(BF16)
    w_out16 = w_out.astype(BF16)
    w_gu16 = w_gate_up.astype(BF16)
    w_d16 = w_down.astype(BF16)
    qg = row(jnp.tile(q_norm_g, n_heads))
    kg = row(jnp.tile(k_norm_g, n_heads))
    lora_up = jnp.zeros((W_RANK + A_RANK, 2 * width), F32)
    lora_up = lora_up.at[:W_RANK, :width].set(w_lora_up).at[W_RANK:, width:].set(a_lora_up)
    w_router_pad = jnp.zeros((d_model, LANES), F32).at[:, :n_experts].set(w_router)
    b_router_pad = jnp.zeros((1, LANES), F32).at[0, :n_experts].set(b_router)
    rwkv_params = (row(mu_shift), row(w0), row(a0), lora_up, row(k_k), row(k_a), row(r_k), row(ln_x_g), row(ln_x_b))

    def front(x, shift0, s0, tile, chunk):
        bsz, t, _ = x.shape
        x2d = x.reshape(bsz * t, d_model)
        p_rw, q16, kb, k16, vb, v16, gate_a, gate_b = _in_proj(
            x2d, row(norm1_g), w_in16, qg, kg, rw_proj=rw_proj, width=width, tm=256)
        seq = lambda a: a.reshape(bsz, t, a.shape[-1])
        p_rw = seq(p_rw)
        mix_a, s_bd = _rwkv(p_rw, seq(gate_a), shift0[:, None, :], _pair_block_diag(s0), *rwkv_params,
                            tile=tile, chunk=chunk)
        return x2d, p_rw, seq(q16), kb, seq(k16), vb, seq(v16), seq(gate_b), mix_a, _pair_diag_blocks(s_bd)

    bp, tp, _ = x_prompt.shape
    bs, ts, _ = x_sample.shape
    xp2d, prw_p, q_p, kb_p, k16_p, vb_p, v16_p, gb_p, mixa_p, wkv_p = front(
        x_prompt, jnp.zeros((bp, rw_proj), F32), jnp.zeros((bp, n_heads, HEAD_DIM, HEAD_DIM), F32), 256, 64)
    xs2d, prw_s, q_s, kb_s, k16_s, vb_s, v16_s, gb_s, mixa_s, wkv_s = front(
        x_sample, state_shift, state_wkv, ts, ts)

    mixb_p = _sb_prompt(q_p, k16_p, v16_p, gb_p, sb_bias.astype(F32), tq=512, tk=128)

    hd = (n_heads, HEAD_DIM)
    n_pool = cache_k.shape[0]
    pages_t = lambda c: c.transpose(0, 2, 3, 1).reshape(n_pool, width, PAGE_SIZE)
    pad_keys = lambda a: jnp.pad(a.reshape(bs, ts, width).transpose(0, 2, 1), ((0, 0), (0, 0), (0, PAGE_SIZE - ts)))
    mixb_s = _sb_sample(q_s, pad_keys(kb_s), pad_keys(vb_s), gb_s, pages_t(cache_k), pages_t(cache_v),
                        page_table.astype(jnp.int32), jnp.repeat(sb_bias.astype(F32), ts)[:, None],
                        pages_per_step=8)

    def back(x2d, mix_a, mix_b):
        m = x2d.shape[0]
        return _out_proj(x2d, mix_a.reshape(m, width), mix_b.reshape(m, width), w_out16, row(norm2_g),
                         w_router_pad, b_router_pad, tm=256)

    x1_p, h2_p, lg_p = back(xp2d, mixa_p, mixb_p)
    x1_s, h2_s, lg_s = back(xs2d, mixa_s, mixb_s)

    x1 = jnp.concatenate([x1_p, x1_s], axis=0)
    h2 = jnp.concatenate([h2_p, h2_s], axis=0)
    logits = jnp.concatenate([lg_p, lg_s], axis=0)[:, :n_experts]
    m_all = x1.shape[0]
    tm_moe = 512
    row_tok, row_gate, tile_expert, n_used, dest = _route(logits, n_experts, tm_moe)
    y_rows = _moe(h2[row_tok], row_gate, tile_expert, n_used, w_gu16, b_gate_up[:, None, :].astype(F32),
                  w_d16, b_down[:, None, :].astype(F32), tm=tm_moe)
    y = x1 + jnp.sum(y_rows[dest.reshape(m_all, TOP_K)], axis=1)

    mp = bp * tp
    return (y[:mp].reshape(bp, tp, d_model), y[mp:].reshape(bs, ts, d_model),
            kb_p.reshape(bp, tp, *hd), vb_p.reshape(bp, tp, *hd),
            kb_s.reshape(bs, ts, *hd), vb_s.reshape(bs, ts, *hd),
            wkv_p, wkv_s, prw_p[:, -1], prw_s[:, -1])
```

```python
import functools
import math

import jax
import jax.numpy as jnp
from jax import lax
from jax.experimental import pallas as pl
from jax.experimental.pallas import tpu as pltpu

F32 = jnp.float32
BF16 = jnp.bfloat16

HEAD_DIM = 64
LANES = 128
PAGE_SIZE = 128
W_RANK = 64
A_RANK = 64
DECAY_SCALE = math.exp(-0.5)
GN_EPS = 64e-5
RMS_EPS = 1e-6
SB_SCALE = HEAD_DIM ** -0.5
TOP_K = 4
SWIGLU_ALPHA = 1.702
SWIGLU_LIMIT = 7.0
VMEM_LIMIT = 56 * 1024 * 1024

NN = (((1,), (0,)), ((), ()))
NT = (((1,), (1,)), ((), ()))
TN = (((0,), (0,)), ((), ()))


def _dot(a, b, dims=NN):
    return lax.dot_general(a, b, dims, preferred_element_type=F32)


def _split2(x):
    hi = x.astype(BF16)
    lo = (x - hi.astype(F32)).astype(BF16)
    return hi, lo


def _split3(x):
    hi = x.astype(BF16)
    r1 = x - hi.astype(F32)
    mid = r1.astype(BF16)
    lo = (r1 - mid.astype(F32)).astype(BF16)
    return hi, mid, lo


def _dot3(a, b, dims=NN):
    (ah, al), (bh, bl) = a, b
    return _dot(ah, bh, dims) + (_dot(ah, bl, dims) + _dot(al, bh, dims))


def _ones_dot(m, x):
    hi, mid, lo = _split3(x)
    return _dot(m, hi) + (_dot(m, mid) + _dot(m, lo))


def _pair_ones():
    r = lax.broadcasted_iota(jnp.int32, (LANES, LANES), 0) // HEAD_DIM
    c = lax.broadcasted_iota(jnp.int32, (LANES, LANES), 1) // HEAD_DIM
    return (r == c).astype(BF16)


def _head_sum(x, bd):
    hi, lo = _split2(x)
    parts = []
    for p in range(x.shape[1] // LANES):
        sl = slice(LANES * p, LANES * (p + 1))
        parts.append(_dot(hi[:, sl], bd) + _dot(lo[:, sl], bd))
    return parts[0] if len(parts) == 1 else jnp.concatenate(parts, axis=1)


def _sigmoid(x):
    return 1.0 / (1.0 + jnp.exp(-x))


def _in_proj_kernel(x_ref, g_ref, w_ref, qg_ref, kg_ref,
                    prw_ref, q16_ref, kb_ref, k16_ref, vb_ref, v16_ref, ga_ref, gb_ref, *, rw_proj, width):
    x = x_ref[...]
    ms = jnp.mean(x * x, axis=-1, keepdims=True)
    h = ((x * lax.rsqrt(ms + RMS_EPS)) * g_ref[...]).astype(BF16)
    bd = _pair_ones()
    o = rw_proj
    prw_ref[...] = _dot(h, w_ref[:, 0:o])
    q = _dot(h, w_ref[:, o:o + width])
    qn = (q * lax.rsqrt(_head_sum(q * q, bd) * (1.0 / HEAD_DIM) + RMS_EPS)) * qg_ref[...]
    q16_ref[...] = (qn * SB_SCALE).astype(BF16)
    k = _dot(h, w_ref[:, o + width:o + 2 * width])
    kn = (k * lax.rsqrt(_head_sum(k * k, bd) * (1.0 / HEAD_DIM) + RMS_EPS)) * kg_ref[...]
    kb_ref[...] = kn
    k16_ref[...] = kn.astype(BF16)
    v = _dot(h, w_ref[:, o + 2 * width:o + 3 * width])
    vb_ref[...] = v
    v16_ref[...] = v.astype(BF16)
    ga_ref[...] = _dot(h, w_ref[:, o + 3 * width:o + 4 * width])
    gb_ref[...] = _dot(h, w_ref[:, o + 4 * width:o + 5 * width])


def _in_proj(x2d, norm_g, w_in16, qg, kg, *, rw_proj, width, tm):
    m, d = x2d.shape
    tm = min(tm, m)
    row = lambda n: pl.BlockSpec((tm, n), lambda i: (i, 0))
    full = lambda a: pl.BlockSpec(a.shape, lambda i: (0,) * a.ndim)
    out_shapes = (
        jax.ShapeDtypeStruct((m, rw_proj), F32),
        jax.ShapeDtypeStruct((m, width), BF16),
        jax.ShapeDtypeStruct((m, width), F32), jax.ShapeDtypeStruct((m, width), BF16),
        jax.ShapeDtypeStruct((m, width), F32), jax.ShapeDtypeStruct((m, width), BF16),
        jax.ShapeDtypeStruct((m, width), F32), jax.ShapeDtypeStruct((m, width), F32),
    )
    return pl.pallas_call(
        functools.partial(_in_proj_kernel, rw_proj=rw_proj, width=width),
        grid=(m // tm,),
        in_specs=[row(d), full(norm_g), full(w_in16), full(qg), full(kg)],
        out_specs=tuple(row(s.shape[1]) for s in out_shapes),
        out_shape=out_shapes,
        compiler_params=pltpu.CompilerParams(dimension_semantics=("arbitrary",), vmem_limit_bytes=VMEM_LIMIT),
        name="in_proj",
    )(x2d, norm_g, w_in16, qg, kg)


def _rwkv_kernel(p_ref, ga_ref, sh0_ref, s0_ref, mu_ref, w0_ref, a0_ref, lora_ref, kk_ref, ka_ref, rk_ref,
                 lng_ref, lnb_ref, mix_ref, sout_ref,
                 carry, state, kap_s, bet_s, gam_s, rho_s, betc_s, gamc_s, v_s, wc_s, o_s, *, chunk, width):
    tb = pl.program_id(1)
    tile = p_ref.shape[1]
    n_pairs = width // LANES

    @pl.when(tb == 0)
    def _():
        carry[...] = sh0_ref[0]
        state[...] = s0_ref[0]

    p = p_ref[0]
    rows = lax.broadcasted_iota(jnp.int32, p.shape, 0)
    prev = jnp.where(rows == 0, carry[...], pltpu.roll(p, 1, axis=0))
    carry[...] = p[tile - 1:tile, :]
    xm = p + (prev - p) * mu_ref[...]
    r = xm[:, 0:width]
    k = xm[:, width:2 * width]
    v = xm[:, 2 * width:3 * width]
    wa = xm[:, 3 * width:3 * width + W_RANK + A_RANK]
    lane = lax.broadcasted_iota(jnp.int32, wa.shape, 1)
    lora_in = jnp.where(lane < W_RANK, jnp.tanh(wa), wa)
    lora = _dot3(_split2(lora_in), _split2(lora_ref[...]))
    lw = -DECAY_SCALE * _sigmoid(w0_ref[...] + lora[:, 0:width])
    a = _sigmoid(a0_ref[...] + lora[:, width:2 * width])
    bd = _pair_ones()
    kk = k * kk_ref[...]
    kk = kk / jnp.maximum(jnp.sqrt(_head_sum(kk * kk, bd)), 1e-12)
    k_h = k * (1.0 + (a - 1.0) * ka_ref[...])
    b = kk * a
    bonus = _head_sum(r * k_h * rk_ref[...], bd) * v

    ti = lax.broadcasted_iota(jnp.int32, (tile, tile), 0)
    tj = lax.broadcasted_iota(jnp.int32, (tile, tile), 1)
    same = (ti // chunk) == (tj // chunk)
    cum = _ones_dot((same & (ti >= tj)).astype(BF16), lw)
    tot = _ones_dot(same.astype(BF16), lw)
    e_neg = jnp.exp(-cum)
    e_c = jnp.exp(tot - cum)
    kap_s[...] = kk * jnp.exp(cum - lw)
    bet_s[...] = b * e_neg
    gam_s[...] = k_h * e_neg
    rho_s[...] = r * jnp.exp(cum)
    betc_s[...] = b * e_c
    gamc_s[...] = k_h * e_c
    v_s[...] = v
    wc_s[...] = jnp.exp(tot)

    c2 = 2 * chunk
    ri = lax.broadcasted_iota(jnp.int32, (c2, c2), 0)
    rj = lax.broadcasted_iota(jnp.int32, (c2, c2), 1)
    same_head = (ri // chunk) == (rj // chunk)
    strict = same_head & (ri > rj)
    incl = same_head & (ri >= rj)
    head0 = lax.broadcasted_iota(jnp.int32, (chunk, LANES), 1) < HEAD_DIM
    n_double = int(math.log2(chunk)) - 1
    pairs = range(n_pairs)

    def stack(x):
        return jnp.concatenate([jnp.where(head0, x, 0.0), jnp.where(head0, 0.0, x)], axis=0)

    def chunk_body(ci, carry_):
        rs = pl.ds(pl.multiple_of(ci * chunk, chunk), chunk)
        cols = [slice(LANES * pr, LANES * (pr + 1)) for pr in pairs]
        bet = [bet_s[rs, ls] for ls in cols]
        gam = [gam_s[rs, ls] for ls in cols]
        s_bd = [state[pr] for pr in pairs]
        kr = [_split2(jnp.concatenate([stack(kap_s[rs, ls]), stack(rho_s[rs, ls])], axis=0)) for ls in cols]
        bet2 = [_split2(jnp.concatenate([x, x], axis=0)) for x in bet]
        gam2 = [_split2(jnp.concatenate([x, x], axis=0)) for x in gam]
        v2 = [_split2(stack(v_s[rs, ls])) for ls in cols]
        ssp = [_split2(x) for x in s_bd]
        p_b = [_dot3(kr[pr], bet2[pr], NT) for pr in pairs]
        p_g = [_dot3(kr[pr], gam2[pr], NT) for pr in pairs]
        xo0 = [_dot3(kr[pr], ssp[pr], NT) for pr in pairs]
        a1 = [_split2(jnp.where(strict, x[0:c2], 0.0)) for x in p_b]
        rb = [_split2(jnp.where(incl, x[c2:2 * c2], 0.0)) for x in p_b]
        a2 = [_split2(jnp.where(strict, x[0:c2], 0.0)) for x in p_g]
        rg = [_split2(jnp.where(incl, x[c2:2 * c2], 0.0)) for x in p_g]
        rhs = [_dot3(a2[pr], v2[pr]) for pr in pairs]
        rhs = [rhs[pr] + xo0[pr][0:c2] for pr in pairs]
        x = [rhs[pr] - _dot3(a1[pr], _split2(rhs[pr])) for pr in pairs]
        ap = a1
        for _ in range(n_double):
            ap = [_split2(_dot3(ap[pr], ap[pr])) for pr in pairs]
            x = [x[pr] + _dot3(ap[pr], _split2(x[pr])) for pr in pairs]
        u2 = [_split2(-x[pr]) for pr in pairs]
        o2 = [_dot3(rb[pr], u2[pr]) + _dot3(rg[pr], v2[pr]) for pr in pairs]
        o2 = [o2[pr] + xo0[pr][c2:2 * c2] for pr in pairs]
        for pr in pairs:
            o_s[rs, cols[pr]] = o2[pr][0:chunk] + o2[pr][chunk:c2]
        s_new = [_dot3(u2[pr], _split2(stack(betc_s[rs, cols[pr]])), TN)
                 + _dot3(v2[pr], _split2(stack(gamc_s[rs, cols[pr]])), TN) for pr in pairs]
        for pr in pairs:
            state[pr] = s_new[pr] + s_bd[pr] * wc_s[rs, cols[pr]][0:1, :]
        return carry_

    lax.fori_loop(0, tile // chunk, chunk_body, 0)

    o = o_s[...]
    mean = _head_sum(o, bd) * (1.0 / HEAD_DIM)
    oc = o - mean
    var = _head_sum(oc * oc, bd) * (1.0 / HEAD_DIM)
    on = (oc * lax.rsqrt(var + GN_EPS)) * lng_ref[...] + lnb_ref[...]
    mix_ref[0] = ((on + bonus) * _sigmoid(ga_ref[0])).astype(BF16)

    @pl.when(tb == pl.num_programs(1) - 1)
    def _():
        sout_ref[0] = state[...]


def _rwkv(p_rw, gate_a, shift0, s0_bd, mu, w0, a0, lora_up, k_k, k_a, r_k, ln_g, ln_b, *, tile, chunk):
    bsz, t, rw_proj = p_rw.shape
    width = gate_a.shape[-1]
    n_pairs = width // LANES
    tok = lambda n: pl.BlockSpec((1, tile, n), lambda b, i: (b, i, 0))
    full = lambda a: pl.BlockSpec(a.shape, lambda b, i: (0,) * a.ndim)
    per_b = lambda a: pl.BlockSpec((1,) + a.shape[1:], lambda b, i: (b,) + (0,) * (a.ndim - 1))
    scr = lambda: pltpu.VMEM((tile, width), F32)
    return pl.pallas_call(
        functools.partial(_rwkv_kernel, chunk=chunk, width=width),
        grid=(bsz, t // tile),
        in_specs=[tok(rw_proj), tok(width), per_b(shift0), per_b(s0_bd), full(mu), full(w0), full(a0),
                  full(lora_up), full(k_k), full(k_a), full(r_k), full(ln_g), full(ln_b)],
        out_specs=(tok(width), per_b(s0_bd)),
        out_shape=(jax.ShapeDtypeStruct((bsz, t, width), BF16), jax.ShapeDtypeStruct(s0_bd.shape, F32)),
        scratch_shapes=[pltpu.VMEM((1, rw_proj), F32), pltpu.VMEM((n_pairs, LANES, LANES), F32)]
                       + [scr() for _ in range(9)],
        compiler_params=pltpu.CompilerParams(dimension_semantics=("arbitrary", "arbitrary"),
                                             vmem_limit_bytes=VMEM_LIMIT),
        name="rwkv",
    )(p_rw, gate_a, shift0, s0_bd, mu, w0, a0, lora_up, k_k, k_a, r_k, ln_g, ln_b)


def _sb_logits(z, valid):
    lg = -(jnp.maximum(z, 0.0) + jnp.log(1.0 + jnp.exp(-jnp.abs(z))))
    return lg if valid is None else jnp.where(valid, lg, 0.0)


def _suffix_tri(n):
    r = lax.broadcasted_iota(jnp.int32, (n, n), 0)
    c = lax.broadcasted_iota(jnp.int32, (n, n), 1)
    return (r >= c).astype(BF16)


def _sb_prompt_kernel(bias_ref, q_ref, k_ref, v_ref, gb_ref, o_ref, acc0_ref, acc1_ref, *, tq, tk):
    hp = pl.program_id(1)
    qi = pl.program_id(2)
    q = q_ref[0]
    head0 = lax.broadcasted_iota(jnp.int32, (tq, LANES), 1) < HEAD_DIM
    qh = (jnp.where(head0, q, jnp.zeros_like(q)), jnp.where(head0, jnp.zeros_like(q), q))
    bias = (bias_ref[2 * hp], bias_ref[2 * hp + 1])
    accs = (acc0_ref, acc1_ref)
    tri = _suffix_tri(tk)
    row = lax.broadcasted_iota(jnp.int32, (tq, tk), 0)
    col = lax.broadcasted_iota(jnp.int32, (tq, tk), 1)
    n_diag = tq // tk
    acc0_ref[...] = jnp.zeros_like(acc0_ref)
    acc1_ref[...] = jnp.zeros_like(acc1_ref)

    def sweep(kbs, cs, masked):
        nb = len(kbs)
        ks = [pl.multiple_of(kb * tk, tk) for kb in kbs]
        kblk = [k_ref[0, pl.ds(s, tk), :] for s in ks]
        vblk = [v_ref[0, pl.ds(s, tk), :] for s in ks]
        valid = [(col + (kb * tk - qi * tq)) < row if masked else None for kb in kbs]
        z = [[_dot(qh[h], kblk[i], NT) + bias[h] for i in range(nb)] for h in range(2)]
        sp = [[_split2(_sb_logits(z[h][i], valid[i])) for i in range(nb)] for h in range(2)]
        incl = [[_dot(sp[h][i][0], tri) + _dot(sp[h][i][1], tri) for i in range(nb)] for h in range(2)]
        new_cs = []
        for h in range(2):
            c = cs[h]
            pv = None
            for i in range(nb):
                w = jnp.exp(z[h][i] + incl[h][i] + c)
                if masked:
                    w = jnp.where(valid[i], w, 0.0)
                c = c + incl[h][i][:, 0:1]
                d = _dot(w.astype(BF16), vblk[i])
                pv = d if pv is None else pv + d
            accs[h][...] += pv
            new_cs.append(c)
        return tuple(new_cs)

    last = (qi + 1) * n_diag - 1
    zero = jnp.zeros((tq, 1), F32)
    cs = sweep([last - i for i in range(n_diag)], (zero, zero), True)
    first_diag = qi * n_diag
    lax.fori_loop(0, qi,
                  lambda j, cs: sweep([first_diag - 1 - n_diag * j - i for i in range(n_diag)], cs, False), cs)
    out = jnp.where(head0, acc0_ref[...], acc1_ref[...])
    o_ref[0] = (out * _sigmoid(gb_ref[0])).astype(BF16)


def _sb_prompt(q16, k16, v16, gate_b, sb_bias, *, tq, tk):
    bsz, t, width = q16.shape
    assert tq % tk == 0 and t % tq == 0
    qspec = pl.BlockSpec((1, tq, LANES), lambda b, hp, i, bias: (b, i, hp))
    kspec = pl.BlockSpec((1, t, LANES), lambda b, hp, i, bias: (b, 0, hp))
    return pl.pallas_call(
        functools.partial(_sb_prompt_kernel, tq=tq, tk=tk),
        grid_spec=pltpu.PrefetchScalarGridSpec(
            num_scalar_prefetch=1,
            grid=(bsz, width // LANES, t // tq),
            in_specs=[qspec, kspec, kspec, qspec],
            out_specs=qspec,
            scratch_shapes=[pltpu.VMEM((tq, LANES), F32), pltpu.VMEM((tq, LANES), F32)],
        ),
        out_shape=jax.ShapeDtypeStruct((bsz, t, width), BF16),
        compiler_params=pltpu.CompilerParams(dimension_semantics=("arbitrary",) * 3, vmem_limit_bytes=VMEM_LIMIT),
        name="sb_prompt",
    )(sb_bias, q16, k16, v16, gate_b)


def _sb_sample_kernel(pt_ref, q_ref, bias_ref, kn_ref, vn_ref, *rest, n_q, pages_per_step, n_heads):
    kp_refs = rest[:pages_per_step]
    vp_refs = rest[pages_per_step:2 * pages_per_step]
    gb_ref, o_ref, qbd_ref, acc_ref, c_ref = rest[2 * pages_per_step:]
    j = pl.program_id(1)
    n_rows = n_heads * n_q
    width = n_heads * HEAD_DIM
    row_head = lax.broadcasted_iota(jnp.int32, (n_rows, width), 0) // n_q
    lane_head = lax.broadcasted_iota(jnp.int32, (n_rows, width), 1) // HEAD_DIM
    own = row_head == lane_head
    tri = _suffix_tri(PAGE_SIZE)

    def blocks(kt_refs, vt_refs, valid):
        z = [_dot(qbd_ref[...], kt[0].astype(BF16)) + bias_ref[...] for kt in kt_refs]
        sp = [_split2(_sb_logits(zi, valid)) for zi in z]
        incl = [_dot(hi, tri) + _dot(lo, tri) for hi, lo in sp]
        c = c_ref[...]
        pv = None
        for i, vt in enumerate(vt_refs):
            w = jnp.exp(z[i] + incl[i] + c)
            if valid is not None:
                w = jnp.where(valid, w, 0.0)
            c = c + incl[i][:, 0:1]
            d = _dot(w.astype(BF16), vt[0].astype(BF16), NT)
            pv = d if pv is None else pv + d
        acc_ref[...] += pv
        c_ref[...] = c

    @pl.when(j == 0)
    def _():
        q = q_ref[0]
        qt = jnp.concatenate([q] * n_heads, axis=0)
        qbd_ref[...] = jnp.where(own, qt, jnp.zeros_like(qt))
        acc_ref[...] = jnp.zeros_like(acc_ref)
        c_ref[...] = jnp.zeros_like(c_ref)
        tq = lax.broadcasted_iota(jnp.int32, (n_rows, PAGE_SIZE), 0) % n_q
        ks = lax.broadcasted_iota(jnp.int32, (n_rows, PAGE_SIZE), 1)
        blocks([kn_ref], [vn_ref], ks < tq)

    @pl.when(j > 0)
    def _():
        blocks(kp_refs, vp_refs, None)

    @pl.when(j == pl.num_programs(1) - 1)
    def _():
        a = jnp.where(own, acc_ref[...], 0.0).reshape(n_heads, n_q, width)
        o_ref[0] = (jnp.sum(a, axis=0) * _sigmoid(gb_ref[0])).astype(BF16)


def _sb_sample(q16, k_new, v_new, gate_b, cache_k, cache_v, page_table, bias_rows, *, pages_per_step):
    bsz, n_q, width = q16.shape
    n_heads = width // HEAD_DIM
    n_pages = page_table.shape[1]
    n_steps = n_pages // pages_per_step

    def page_spec(p):
        def index(b, j, pt):
            step = jnp.maximum(j - 1, 0)
            return (pt[b, n_pages - 1 - (step * pages_per_step + p)], 0, 0)
        return pl.BlockSpec((1, width, PAGE_SIZE), index)

    new_keys = pl.BlockSpec((1, width, PAGE_SIZE), lambda b, j, pt: (b, 0, 0))
    seq = lambda n: pl.BlockSpec((1, n, width), lambda b, j, pt: (b, 0, 0))
    return pl.pallas_call(
        functools.partial(_sb_sample_kernel, n_q=n_q, pages_per_step=pages_per_step, n_heads=n_heads),
        grid_spec=pltpu.PrefetchScalarGridSpec(
            num_scalar_prefetch=1,
            grid=(bsz, n_steps + 1),
            in_specs=[seq(n_q), pl.BlockSpec(bias_rows.shape, lambda b, j, pt: (0, 0)), new_keys, new_keys]
                     + [page_spec(p) for p in range(pages_per_step)] * 2 + [seq(n_q)],
            out_specs=seq(n_q),
            scratch_shapes=[pltpu.VMEM((n_heads * n_q, width), BF16), pltpu.VMEM((n_heads * n_q, width), F32),
                            pltpu.VMEM((n_heads * n_q, 1), F32)],
        ),
        out_shape=jax.ShapeDtypeStruct((bsz, n_q, width), BF16),
        compiler_params=pltpu.CompilerParams(dimension_semantics=("arbitrary", "arbitrary"),
                                             vmem_limit_bytes=VMEM_LIMIT),
        name="sb_sample",
    )(page_table, q16, bias_rows, k_new, v_new, *([cache_k] * pages_per_step), *([cache_v] * pages_per_step),
      gate_b)


def _out_proj_kernel(x_ref, ma_ref, mb_ref, w_ref, g_ref, wr_ref, br_ref, x1_ref, h2_ref, lg_ref):
    half = ma_ref.shape[1]
    x1 = x_ref[...] + (_dot(ma_ref[...], w_ref[0:half, :]) + _dot(mb_ref[...], w_ref[half:2 * half, :]))
    x1_ref[...] = x1
    ms = jnp.mean(x1 * x1, axis=-1, keepdims=True)
    h2 = (x1 * lax.rsqrt(ms + RMS_EPS)) * g_ref[...]
    h2_ref[...] = h2.astype(BF16)
    lg_ref[...] = _dot3(_split2(h2), _split2(wr_ref[...])) + br_ref[...]


def _out_proj(x2d, mix_a, mix_b, w_out16, norm_g, w_router_pad, b_router_pad, *, tm):
    m, d = x2d.shape
    tm = min(tm, m)
    row = lambda n: pl.BlockSpec((tm, n), lambda i: (i, 0))
    full = lambda a: pl.BlockSpec(a.shape, lambda i: (0,) * a.ndim)
    n_lg = w_router_pad.shape[1]
    return pl.pallas_call(
        _out_proj_kernel,
        grid=(m // tm,),
        in_specs=[row(d), row(mix_a.shape[1]), row(mix_b.shape[1]), full(w_out16), full(norm_g),
                  full(w_router_pad), full(b_router_pad)],
        out_specs=(row(d), row(d), row(n_lg)),
        out_shape=(jax.ShapeDtypeStruct((m, d), F32), jax.ShapeDtypeStruct((m, d), BF16),
                   jax.ShapeDtypeStruct((m, n_lg), F32)),
        compiler_params=pltpu.CompilerParams(dimension_semantics=("arbitrary",), vmem_limit_bytes=VMEM_LIMIT),
        name="out_proj",
    )(x2d, mix_a, mix_b, w_out16, norm_g, w_router_pad, b_router_pad)


def _moe_kernel(te_ref, nu_ref, x_ref, gate_ref, wgu_ref, bgu_ref, wd_ref, bd_ref, o_ref):
    t = pl.program_id(0)
    d_ff = wd_ref.shape[1]

    @pl.when(t < nu_ref[0])
    def _():
        gu = _dot(x_ref[...], wgu_ref[0]) + bgu_ref[0]
        g_lin = jnp.minimum(gu[:, 0:d_ff], SWIGLU_LIMIT)
        u_lin = jnp.clip(gu[:, d_ff:2 * d_ff], -SWIGLU_LIMIT, SWIGLU_LIMIT)
        act = g_lin * _sigmoid(SWIGLU_ALPHA * g_lin) * (u_lin + 1.0)
        y = _dot(act.astype(BF16), wd_ref[0]) + bd_ref[0]
        o_ref[...] = y * gate_ref[...]

    @pl.when(t >= nu_ref[0])
    def _():
        o_ref[...] = jnp.zeros_like(o_ref)


def _moe(xs16, row_gate, tile_expert, n_used, w_gu16, b_gu, w_d16, b_d, *, tm):
    r, d = xs16.shape
    d_ff = w_d16.shape[1]
    n_tiles = r // tm
    row = lambda n: pl.BlockSpec((tm, n), lambda t, te, nu: (t, 0))
    ex = lambda a: pl.BlockSpec((1,) + a.shape[1:], lambda t, te, nu: (te[t], 0, 0))
    return pl.pallas_call(
        _moe_kernel,
        grid_spec=pltpu.PrefetchScalarGridSpec(
            num_scalar_prefetch=2,
            grid=(n_tiles,),
            in_specs=[row(d), row(1), ex(w_gu16), ex(b_gu), ex(w_d16), ex(b_d)],
            out_specs=row(d),
        ),
        out_shape=jax.ShapeDtypeStruct((r, d), F32),
        compiler_params=pltpu.CompilerParams(dimension_semantics=("arbitrary",), vmem_limit_bytes=VMEM_LIMIT),
        name="moe",
    )(tile_expert, n_used, xs16, row_gate, w_gu16, b_gu, w_d16, b_d)


def _route(logits, n_experts, tm):
    m = logits.shape[0]
    n_assign = m * TOP_K
    top_logit, top_e = lax.top_k(logits, TOP_K)
    gate = jax.nn.softmax(top_logit, axis=-1).reshape(n_assign)
    e_flat = top_e.reshape(n_assign).astype(jnp.int32)
    onehot = (e_flat[:, None] == jnp.arange(n_experts, dtype=jnp.int32)[None, :]).astype(jnp.int32)
    csum = jnp.cumsum(onehot, axis=0)
    rank = jnp.sum((csum - onehot) * onehot, axis=1)
    counts = csum[-1]
    padded = (counts + tm - 1) // tm * tm
    padded_end = jnp.cumsum(padded)
    dest = (padded_end - padded)[e_flat] + rank
    n_tiles = (n_assign + n_experts * (tm - 1)) // tm + 1
    n_rows = n_tiles * tm
    src = jnp.zeros((n_rows,), jnp.int32).at[dest].set(jnp.arange(1, n_assign + 1, dtype=jnp.int32))
    filled = src > 0
    src = jnp.maximum(src - 1, 0)
    row_tok = jnp.where(filled, src // TOP_K, 0)
    row_gate = jnp.where(filled, gate[src], 0.0)
    tile_expert = jnp.minimum(
        jnp.searchsorted(padded_end, jnp.arange(n_tiles, dtype=jnp.int32) * tm, side='right'),
        n_experts - 1).astype(jnp.int32)
    n_used = (padded_end[-1] // tm).astype(jnp.int32).reshape(1)
    return row_tok, row_gate[:, None], tile_expert, n_used, dest


def _pair_block_diag(s):
    bsz, h = s.shape[:2]
    s = s.reshape(bsz, h // 2, 2, HEAD_DIM, HEAD_DIM)
    eye = jnp.eye(2, dtype=s.dtype)
    return jnp.einsum('bpivk,ij->bpivjk', s, eye).reshape(bsz, h // 2, LANES, LANES)


def _pair_diag_blocks(s_bd):
    bsz, n_pairs = s_bd.shape[:2]
    s = s_bd.reshape(bsz, n_pairs, 2, HEAD_DIM, 2, HEAD_DIM)
    return jnp.stack([s[:, :, 0, :, 0, :], s[:, :, 1, :, 1, :]], axis=2).reshape(
        bsz, n_pairs * 2, HEAD_DIM, HEAD_DIM)


def kernel(x_prompt, x_sample, cache_k, cache_v, page_table, state_wkv, state_shift, norm1_g, w_in, mu_shift, w0, w_lora_up, a0, a_lora_up, k_k, k_a, r_k, ln_x_g, ln_x_b, q_norm_g, k_norm_g, sb_bias, w_out, norm2_g, w_router, b_router, w_gate_up, b_gate_up, w_down, b_down):
    d_model = x_prompt.shape[-1]
    rw_proj = mu_shift.shape[0]
    width = w0.shape[0]
    n_heads = width // HEAD_DIM
    n_experts = w_router.shape[1]
    row = lambda a: a.reshape(1, -1).astype(F32)

    w_in16 = w_in.astype(BF16)
    w_out16 = w_out.astype(BF16)
    w_gu16 = w_gate_up.astype(BF16)
    w_d16 = w_down.astype(BF16)
    qg = row(jnp.tile(q_norm_g, n_heads))
    kg = row(jnp.tile(k_norm_g, n_heads))
    lora_up = jnp.zeros((W_RANK + A_RANK, 2 * width), F32)
    lora_up = lora_up.at[:W_RANK, :width].set(w_lora_up).at[W_RANK:, width:].set(a_lora_up)
    w_router_pad = jnp.zeros((d_model, LANES), F32).at[:, :n_experts].set(w_router)
    b_router_pad = jnp.zeros((1, LANES), F32).at[0, :n_experts].set(b_router)
    rwkv_params = (row(mu_shift), row(w0), row(a0), lora_up, row(k_k), row(k_a), row(r_k), row(ln_x_g), row(ln_x_b))

    def front(x, shift0, s0, tile, chunk):
        bsz, t, _ = x.shape
        x2d = x.reshape(bsz * t, d_model)
        p_rw, q16, kb, k16, vb, v16, gate_a, gate_b = _in_proj(
            x2d, row(norm1_g), w_in16, qg, kg, rw_proj=rw_proj, width=width, tm=256)
        seq = lambda a: a.reshape(bsz, t, a.shape[-1])
        p_rw = seq(p_rw)
        mix_a, s_bd = _rwkv(p_rw, seq(gate_a), shift0[:, None, :], _pair_block_diag(s0), *rwkv_params,
                            tile=tile, chunk=chunk)
        return x2d, p_rw, seq(q16), kb, seq(k16), vb, seq(v16), seq(gate_b), mix_a, _pair_diag_blocks(s_bd)

    bp, tp, _ = x_prompt.shape
    bs, ts, _ = x_sample.shape
    xp2d, prw_p, q_p, kb_p, k16_p, vb_p, v16_p, gb_p, mixa_p, wkv_p = front(
        x_prompt, jnp.zeros((bp, rw_proj), F32), jnp.zeros((bp, n_heads, HEAD_DIM, HEAD_DIM), F32), 256, 64)
    xs2d, prw_s, q_s, kb_s, k16_s, vb_s, v16_s, gb_s, mixa_s, wkv_s = front(
        x_sample, state_shift, state_wkv, ts, ts)

    mixb_p = _sb_prompt(q_p, k16_p, v16_p, gb_p, sb_bias.astype(F32), tq=512, tk=128)

    hd = (n_heads, HEAD_DIM)
    n_pool = cache_k.shape[0]
    pages_t = lambda c: c.transpose(0, 2, 3, 1).reshape(n_pool, width, PAGE_SIZE)
    pad_keys = lambda a: jnp.pad(a.reshape(bs, ts, width).transpose(0, 2, 1), ((0, 0), (0, 0), (0, PAGE_SIZE - ts)))
    mixb_s = _sb_sample(q_s, pad_keys(kb_s), pad_keys(vb_s), gb_s, pages_t(cache_k), pages_t(cache_v),
                        page_table.astype(jnp.int32), jnp.repeat(sb_bias.astype(F32), ts)[:, None],
                        pages_per_step=8)

    def back(x2d, mix_a, mix_b):
        m = x2d.shape[0]
        return _out_proj(x2d, mix_a.reshape(m, width), mix_b.reshape(m, width), w_out16, row(norm2_g),
                         w_router_pad, b_router_pad, tm=256)

    x1_p, h2_p, lg_p = back(xp2d, mixa_p, mixb_p)
    x1_s, h2_s, lg_s = back(xs2d, mixa_s, mixb_s)

    x1 = jnp.concatenate([x1_p, x1_s], axis=0)
    h2 = jnp.concatenate([h2_p, h2_s], axis=0)
    logits = jnp.concatenate([lg_p, lg_s], axis=0)[:, :n_experts]
    m_all = x1.shape[0]
    tm_moe = 512
    row_tok, row_gate, tile_expert, n_used, dest = _route(logits, n_experts, tm_moe)
    y_rows = _moe(h2[row_tok], row_gate, tile_expert, n_used, w_gu16, b_gate_up[:, None, :].astype(F32),
                  w_d16, b_down[:, None, :].astype(F32), tm=tm_moe)
    y = x1 + jnp.sum(y_rows[dest.reshape(m_all, TOP_K)], axis=1)

    mp = bp * tp
    return (y[:mp].reshape(bp, tp, d_model), y[mp:].reshape(bs, ts, d_model),
            kb_p.reshape(bp, tp, *hd), vb_p.reshape(bp, tp, *hd),
            kb_s.reshape(bs, ts, *hd), vb_s.reshape(bs, ts, *hd),
            wkv_p, wkv_s, prw_p[:, -1], prw_s[:, -1])
```

```python
import functools
import math

import jax
import jax.numpy as jnp
from jax import lax
from jax.experimental import pallas as pl
from jax.experimental.pallas import tpu as pltpu

F32 = jnp.float32
BF16 = jnp.bfloat16

HEAD_DIM = 64
LANES = 128
PAGE_SIZE = 128
W_RANK = 64
A_RANK = 64
DECAY_SCALE = math.exp(-0.5)
GN_EPS = 64e-5
RMS_EPS = 1e-6
SB_SCALE = HEAD_DIM ** -0.5
TOP_K = 4
SWIGLU_ALPHA = 1.702
SWIGLU_LIMIT = 7.0
VMEM_LIMIT = 56 * 1024 * 1024

NN = (((1,), (0,)), ((), ()))
NT = (((1,), (1,)), ((), ()))
TN = (((0,), (0,)), ((), ()))


def _dot(a, b, dims=NN):
    return lax.dot_general(a, b, dims, preferred_element_type=F32)


def _split2(x):
    hi = x.astype(BF16)
    lo = (x - hi.astype(F32)).astype(BF16)
    return hi, lo


def _split3(x):
    hi = x.astype(BF16)
    r1 = x - hi.astype(F32)
    mid = r1.astype(BF16)
    lo = (r1 - mid.astype(F32)).astype(BF16)
    return hi, mid, lo


def _dot3(a, b, dims=NN):
    (ah, al), (bh, bl) = a, b
    return _dot(ah, bh, dims) + (_dot(ah, bl, dims) + _dot(al, bh, dims))


def _dot2(a, b, dims=NN):
    (ah, al), (bh, bl) = a, b
    return _dot(ah, bh, dims) + _dot(ah, bl, dims)


def _ones_dot(m, x):
    hi, mid, lo = _split3(x)
    return _dot(m, hi) + (_dot(m, mid) + _dot(m, lo))


def _pair_ones():
    r = lax.broadcasted_iota(jnp.int32, (LANES, LANES), 0) // HEAD_DIM
    c = lax.broadcasted_iota(jnp.int32, (LANES, LANES), 1) // HEAD_DIM
    return (r == c).astype(BF16)


def _head_sum(x, bd):
    hi, lo = _split2(x)
    parts = []
    for p in range(x.shape[1] // LANES):
        sl = slice(LANES * p, LANES * (p + 1))
        parts.append(_dot(hi[:, sl], bd) + _dot(lo[:, sl], bd))
    return parts[0] if len(parts) == 1 else jnp.concatenate(parts, axis=1)


def _sigmoid(x):
    return 1.0 / (1.0 + jnp.exp(-x))


def _in_proj_kernel(x_ref, g_ref, w_ref, qg_ref, kg_ref,
                    prw_ref, q16_ref, kb_ref, k16_ref, vb_ref, v16_ref, ga_ref, gb_ref, *, rw_proj, width):
    x = x_ref[...]
    ms = jnp.mean(x * x, axis=-1, keepdims=True)
    h = ((x * lax.rsqrt(ms + RMS_EPS)) * g_ref[...]).astype(BF16)
    bd = _pair_ones()
    o = rw_proj
    prw_ref[...] = _dot(h, w_ref[:, 0:o])
    q = _dot(h, w_ref[:, o:o + width])
    qn = (q * lax.rsqrt(_head_sum(q * q, bd) * (1.0 / HEAD_DIM) + RMS_EPS)) * qg_ref[...]
    q16_ref[...] = (qn * SB_SCALE).astype(BF16)
    k = _dot(h, w_ref[:, o + width:o + 2 * width])
    kn = (k * lax.rsqrt(_head_sum(k * k, bd) * (1.0 / HEAD_DIM) + RMS_EPS)) * kg_ref[...]
    kb_ref[...] = kn
    k16_ref[...] = kn.astype(BF16)
    v = _dot(h, w_ref[:, o + 2 * width:o + 3 * width])
    vb_ref[...] = v
    v16_ref[...] = v.astype(BF16)
    ga_ref[...] = _dot(h, w_ref[:, o + 3 * width:o + 4 * width])
    gb_ref[...] = _dot(h, w_ref[:, o + 4 * width:o + 5 * width])


def _in_proj(x2d, norm_g, w_in16, qg, kg, *, rw_proj, width, tm):
    m, d = x2d.shape
    tm = min(tm, m)
    row = lambda n: pl.BlockSpec((tm, n), lambda i: (i, 0))
    full = lambda a: pl.BlockSpec(a.shape, lambda i: (0,) * a.ndim)
    out_shapes = (
        jax.ShapeDtypeStruct((m, rw_proj), F32),
        jax.ShapeDtypeStruct((m, width), BF16),
        jax.ShapeDtypeStruct((m, width), F32), jax.ShapeDtypeStruct((m, width), BF16),
        jax.ShapeDtypeStruct((m, width), F32), jax.ShapeDtypeStruct((m, width), BF16),
        jax.ShapeDtypeStruct((m, width), F32), jax.ShapeDtypeStruct((m, width), F32),
    )
    return pl.pallas_call(
        functools.partial(_in_proj_kernel, rw_proj=rw_proj, width=width),
        grid=(m // tm,),
        in_specs=[row(d), full(norm_g), full(w_in16), full(qg), full(kg)],
        out_specs=tuple(row(s.shape[1]) for s in out_shapes),
        out_shape=out_shapes,
        compiler_params=pltpu.CompilerParams(dimension_semantics=("arbitrary",), vmem_limit_bytes=VMEM_LIMIT),
        name="in_proj",
    )(x2d, norm_g, w_in16, qg, kg)


def _rwkv_kernel(p_ref, ga_ref, sh0_ref, s0_ref, mu_ref, w0_ref, a0_ref, lora_ref, kk_ref, ka_ref, rk_ref,
                 lng_ref, lnb_ref, mix_ref, sout_ref,
                 carry, state, kap_s, bet_s, gam_s, rho_s, betc_s, gamc_s, v_s, wc_s, o_s, *, chunk, width):
    tb = pl.program_id(1)
    tile = p_ref.shape[1]
    n_pairs = width // LANES

    @pl.when(tb == 0)
    def _():
        carry[...] = sh0_ref[0]
        state[...] = s0_ref[0]

    p = p_ref[0]
    rows = lax.broadcasted_iota(jnp.int32, p.shape, 0)
    prev = jnp.where(rows == 0, carry[...], pltpu.roll(p, 1, axis=0))
    carry[...] = p[tile - 1:tile, :]
    xm = p + (prev - p) * mu_ref[...]
    r = xm[:, 0:width]
    k = xm[:, width:2 * width]
    v = xm[:, 2 * width:3 * width]
    wa = xm[:, 3 * width:3 * width + W_RANK + A_RANK]
    lane = lax.broadcasted_iota(jnp.int32, wa.shape, 1)
    lora_in = jnp.where(lane < W_RANK, jnp.tanh(wa), wa)
    lora = _dot3(_split2(lora_in), _split2(lora_ref[...]))
    lw = -DECAY_SCALE * _sigmoid(w0_ref[...] + lora[:, 0:width])
    a = _sigmoid(a0_ref[...] + lora[:, width:2 * width])
    bd = _pair_ones()
    kk = k * kk_ref[...]
    kk = kk / jnp.maximum(jnp.sqrt(_head_sum(kk * kk, bd)), 1e-12)
    k_h = k * (1.0 + (a - 1.0) * ka_ref[...])
    b = kk * a
    bonus = _head_sum(r * k_h * rk_ref[...], bd) * v

    ti = lax.broadcasted_iota(jnp.int32, (tile, tile), 0)
    tj = lax.broadcasted_iota(jnp.int32, (tile, tile), 1)
    same = (ti // chunk) == (tj // chunk)
    cum = _ones_dot((same & (ti >= tj)).astype(BF16), lw)
    tot = _ones_dot(same.astype(BF16), lw)
    e_neg = jnp.exp(-cum)
    e_c = jnp.exp(tot - cum)
    kap_s[...] = kk * jnp.exp(cum - lw)
    bet_s[...] = b * e_neg
    gam_s[...] = k_h * e_neg
    rho_s[...] = r * jnp.exp(cum)
    betc_s[...] = b * e_c
    gamc_s[...] = k_h * e_c
    v_s[...] = v
    wc_s[...] = jnp.exp(tot)

    c2 = 2 * chunk
    ri = lax.broadcasted_iota(jnp.int32, (c2, c2), 0)
    rj = lax.broadcasted_iota(jnp.int32, (c2, c2), 1)
    same_head = (ri // chunk) == (rj // chunk)
    strict = same_head & (ri > rj)
    incl = same_head & (ri >= rj)
    head0 = lax.broadcasted_iota(jnp.int32, (chunk, LANES), 1) < HEAD_DIM
    n_double = int(math.log2(chunk)) - 1
    pairs = range(n_pairs)

    def stack(x):
        return jnp.concatenate([jnp.where(head0, x, 0.0), jnp.where(head0, 0.0, x)], axis=0)

    def chunk_body(ci, carry_):
        rs = pl.ds(pl.multiple_of(ci * chunk, chunk), chunk)
        cols = [slice(LANES * pr, LANES * (pr + 1)) for pr in pairs]
        bet = [bet_s[rs, ls] for ls in cols]
        gam = [gam_s[rs, ls] for ls in cols]
        s_bd = [state[pr] for pr in pairs]
        kr = [_split2(jnp.concatenate([stack(kap_s[rs, ls]), stack(rho_s[rs, ls])], axis=0)) for ls in cols]
        bet2 = [_split2(jnp.concatenate([x, x], axis=0)) for x in bet]
        gam2 = [_split2(jnp.concatenate([x, x], axis=0)) for x in gam]
        v2 = [_split2(stack(v_s[rs, ls])) for ls in cols]
        ssp = [_split2(x) for x in s_bd]
        p_b = [_dot3(kr[pr], bet2[pr], NT) for pr in pairs]
        p_g = [_dot3(kr[pr], gam2[pr], NT) for pr in pairs]
        xo0 = [_dot3(kr[pr], ssp[pr], NT) for pr in pairs]
        a1 = [_split2(jnp.where(strict, x[0:c2], 0.0)) for x in p_b]
        rb = [_split2(jnp.where(incl, x[c2:2 * c2], 0.0)) for x in p_b]
        a2 = [_split2(jnp.where(strict, x[0:c2], 0.0)) for x in p_g]
        rg = [_split2(jnp.where(incl, x[c2:2 * c2], 0.0)) for x in p_g]
        rhs = [_dot3(a2[pr], v2[pr]) for pr in pairs]
        rhs = [rhs[pr] + xo0[pr][0:c2] for pr in pairs]
        x = [rhs[pr] - _dot3(a1[pr], _split2(rhs[pr])) for pr in pairs]
        ap = a1
        for _ in range(n_double):
            ap = [_split2(_dot2(ap[pr], ap[pr])) for pr in pairs]
            x = [x[pr] + _dot2(ap[pr], _split2(x[pr])) for pr in pairs]
        u2 = [_split2(-x[pr]) for pr in pairs]
        o2 = [_dot3(rb[pr], u2[pr]) + _dot3(rg[pr], v2[pr]) for pr in pairs]
        o2 = [o2[pr] + xo0[pr][c2:2 * c2] for pr in pairs]
        for pr in pairs:
            o_s[rs, cols[pr]] = o2[pr][0:chunk] + o2[pr][chunk:c2]
        s_new = [_dot3(u2[pr], _split2(stack(betc_s[rs, cols[pr]])), TN)
                 + _dot3(v2[pr], _split2(stack(gamc_s[rs, cols[pr]])), TN) for pr in pairs]
        for pr in pairs:
            state[pr] = s_new[pr] + s_bd[pr] * wc_s[rs, cols[pr]][0:1, :]
        return carry_

    lax.fori_loop(0, tile // chunk, chunk_body, 0)

    o = o_s[...]
    mean = _head_sum(o, bd) * (1.0 / HEAD_DIM)
    oc = o - mean
    var = _head_sum(oc * oc, bd) * (1.0 / HEAD_DIM)
    on = (oc * lax.rsqrt(var + GN_EPS)) * lng_ref[...] + lnb_ref[...]
    mix_ref[0] = ((on + bonus) * _sigmoid(ga_ref[0])).astype(BF16)

    @pl.when(tb == pl.num_programs(1) - 1)
    def _():
        sout_ref[0] = state[...]


def _rwkv(p_rw, gate_a, shift0, s0_bd, mu, w0, a0, lora_up, k_k, k_a, r_k, ln_g, ln_b, *, tile, chunk):
    bsz, t, rw_proj = p_rw.shape
    width = gate_a.shape[-1]
    n_pairs = width // LANES
    tok = lambda n: pl.BlockSpec((1, tile, n), lambda b, i: (b, i, 0))
    full = lambda a: pl.BlockSpec(a.shape, lambda b, i: (0,) * a.ndim)
    per_b = lambda a: pl.BlockSpec((1,) + a.shape[1:], lambda b, i: (b,) + (0,) * (a.ndim - 1))
    scr = lambda: pltpu.VMEM((tile, width), F32)
    return pl.pallas_call(
        functools.partial(_rwkv_kernel, chunk=chunk, width=width),
        grid=(bsz, t // tile),
        in_specs=[tok(rw_proj), tok(width), per_b(shift0), per_b(s0_bd), full(mu), full(w0), full(a0),
                  full(lora_up), full(k_k), full(k_a), full(r_k), full(ln_g), full(ln_b)],
        out_specs=(tok(width), per_b(s0_bd)),
        out_shape=(jax.ShapeDtypeStruct((bsz, t, width), BF16), jax.ShapeDtypeStruct(s0_bd.shape, F32)),
        scratch_shapes=[pltpu.VMEM((1, rw_proj), F32), pltpu.VMEM((n_pairs, LANES, LANES), F32)]
                       + [scr() for _ in range(9)],
        compiler_params=pltpu.CompilerParams(dimension_semantics=("arbitrary", "arbitrary"),
                                             vmem_limit_bytes=VMEM_LIMIT),
        name="rwkv",
    )(p_rw, gate_a, shift0, s0_bd, mu, w0, a0, lora_up, k_k, k_a, r_k, ln_g, ln_b)


def _sb_logits(z, valid):
    lg = -(jnp.maximum(z, 0.0) + jnp.log(1.0 + jnp.exp(-jnp.abs(z))))
    return lg if valid is None else jnp.where(valid, lg, 0.0)


def _suffix_tri(n):
    r = lax.broadcasted_iota(jnp.int32, (n, n), 0)
    c = lax.broadcasted_iota(jnp.int32, (n, n), 1)
    return (r >= c).astype(BF16)


def _sb_prompt_kernel(bias_ref, q_ref, k_ref, v_ref, gb_ref, o_ref, acc0_ref, acc1_ref, *, tq, tk):
    hp = pl.program_id(1)
    qi = pl.program_id(2)
    q = q_ref[0]
    head0 = lax.broadcasted_iota(jnp.int32, (tq, LANES), 1) < HEAD_DIM
    qh = (jnp.where(head0, q, jnp.zeros_like(q)), jnp.where(head0, jnp.zeros_like(q), q))
    bias = (bias_ref[2 * hp], bias_ref[2 * hp + 1])
    accs = (acc0_ref, acc1_ref)
    tri = _suffix_tri(tk)
    row = lax.broadcasted_iota(jnp.int32, (tq, tk), 0)
    col = lax.broadcasted_iota(jnp.int32, (tq, tk), 1)
    n_diag = tq // tk
    acc0_ref[...] = jnp.zeros_like(acc0_ref)
    acc1_ref[...] = jnp.zeros_like(acc1_ref)

    def sweep(kbs, cs, masked):
        nb = len(kbs)
        ks = [pl.multiple_of(kb * tk, tk) for kb in kbs]
        kblk = [k_ref[0, pl.ds(s, tk), :] for s in ks]
        vblk = [v_ref[0, pl.ds(s, tk), :] for s in ks]
        valid = [(col + (kb * tk - qi * tq)) < row if masked else None for kb in kbs]
        z = [[_dot(qh[h], kblk[i], NT) + bias[h] for i in range(nb)] for h in range(2)]
        sp = [[_split2(_sb_logits(z[h][i], valid[i])) for i in range(nb)] for h in range(2)]
        incl = [[_dot(sp[h][i][0], tri) + _dot(sp[h][i][1], tri) for i in range(nb)] for h in range(2)]
        new_cs = []
        for h in range(2):
            c = cs[h]
            pv = None
            for i in range(nb):
                w = jnp.exp(z[h][i] + incl[h][i] + c)
                if masked:
                    w = jnp.where(valid[i], w, 0.0)
                c = c + incl[h][i][:, 0:1]
                d = _dot(w.astype(BF16), vblk[i])
                pv = d if pv is None else pv + d
            accs[h][...] += pv
            new_cs.append(c)
        return tuple(new_cs)

    last = (qi + 1) * n_diag - 1
    zero = jnp.zeros((tq, 1), F32)
    cs = sweep([last - i for i in range(n_diag)], (zero, zero), True)
    first_diag = qi * n_diag
    lax.fori_loop(0, qi,
                  lambda j, cs: sweep([first_diag - 1 - n_diag * j - i for i in range(n_diag)], cs, False), cs)
    out = jnp.where(head0, acc0_ref[...], acc1_ref[...])
    o_ref[0] = (out * _sigmoid(gb_ref[0])).astype(BF16)


def _sb_prompt(q16, k16, v16, gate_b, sb_bias, *, tq, tk):
    bsz, t, width = q16.shape
    assert tq % tk == 0 and t % tq == 0
    qspec = pl.BlockSpec((1, tq, LANES), lambda b, hp, i, bias: (b, i, hp))
    kspec = pl.BlockSpec((1, t, LANES), lambda b, hp, i, bias: (b, 0, hp))
    return pl.pallas_call(
        functools.partial(_sb_prompt_kernel, tq=tq, tk=tk),
        grid_spec=pltpu.PrefetchScalarGridSpec(
            num_scalar_prefetch=1,
            grid=(bsz, width // LANES, t // tq),
            in_specs=[qspec, kspec, kspec, qspec],
            out_specs=qspec,
            scratch_shapes=[pltpu.VMEM((tq, LANES), F32), pltpu.VMEM((tq, LANES), F32)],
        ),
        out_shape=jax.ShapeDtypeStruct((bsz, t, width), BF16),
        compiler_params=pltpu.CompilerParams(dimension_semantics=("arbitrary",) * 3, vmem_limit_bytes=VMEM_LIMIT),
        name="sb_prompt",
    )(sb_bias, q16, k16, v16, gate_b)


def _sb_sample_kernel(pt_ref, q_ref, bias_ref, kn_ref, vn_ref, *rest, n_q, pages_per_step, n_heads):
    kp_refs = rest[:pages_per_step]
    vp_refs = rest[pages_per_step:2 * pages_per_step]
    gb_ref, o_ref, qbd_ref, acc_ref, c_ref = rest[2 * pages_per_step:]
    j = pl.program_id(1)
    n_rows = n_heads * n_q
    width = n_heads * HEAD_DIM
    row_head = lax.broadcasted_iota(jnp.int32, (n_rows, width), 0) // n_q
    lane_head = lax.broadcasted_iota(jnp.int32, (n_rows, width), 1) // HEAD_DIM
    own = row_head == lane_head
    tri = _suffix_tri(PAGE_SIZE)

    def blocks(kt_refs, vt_refs, valid):
        z = [_dot(qbd_ref[...], kt[0].astype(BF16)) + bias_ref[...] for kt in kt_refs]
        sp = [_split2(_sb_logits(zi, valid)) for zi in z]
        incl = [_dot(hi, tri) + _dot(lo, tri) for hi, lo in sp]
        c = c_ref[...]
        pv = None
        for i, vt in enumerate(vt_refs):
            w = jnp.exp(z[i] + incl[i] + c)
            if valid is not None:
                w = jnp.where(valid, w, 0.0)
            c = c + incl[i][:, 0:1]
            d = _dot(w.astype(BF16), vt[0].astype(BF16), NT)
            pv = d if pv is None else pv + d
        acc_ref[...] += pv
        c_ref[...] = c

    @pl.when(j == 0)
    def _():
        q = q_ref[0]
        qt = jnp.concatenate([q] * n_heads, axis=0)
        qbd_ref[...] = jnp.where(own, qt, jnp.zeros_like(qt))
        acc_ref[...] = jnp.zeros_like(acc_ref)
        c_ref[...] = jnp.zeros_like(c_ref)
        tq = lax.broadcasted_iota(jnp.int32, (n_rows, PAGE_SIZE), 0) % n_q
        ks = lax.broadcasted_iota(jnp.int32, (n_rows, PAGE_SIZE), 1)
        blocks([kn_ref], [vn_ref], ks < tq)

    @pl.when(j > 0)
    def _():
        blocks(kp_refs, vp_refs, None)

    @pl.when(j == pl.num_programs(1) - 1)
    def _():
        a = jnp.where(own, acc_ref[...], 0.0).reshape(n_heads, n_q, width)
        o_ref[0] = (jnp.sum(a, axis=0) * _sigmoid(gb_ref[0])).astype(BF16)


def _sb_sample(q16, k_new, v_new, gate_b, cache_k, cache_v, page_table, bias_rows, *, pages_per_step):
    bsz, n_q, width = q16.shape
    n_heads = width // HEAD_DIM
    n_pages = page_table.shape[1]
    n_steps = n_pages // pages_per_step

    def page_spec(p):
        def index(b, j, pt):
            step = jnp.maximum(j - 1, 0)
            return (pt[b, n_pages - 1 - (step * pages_per_step + p)], 0, 0)
        return pl.BlockSpec((1, width, PAGE_SIZE), index)

    new_keys = pl.BlockSpec((1, width, PAGE_SIZE), lambda b, j, pt: (b, 0, 0))
    seq = lambda n: pl.BlockSpec((1, n, width), lambda b, j, pt: (b, 0, 0))
    return pl.pallas_call(
        functools.partial(_sb_sample_kernel, n_q=n_q, pages_per_step=pages_per_step, n_heads=n_heads),
        grid_spec=pltpu.PrefetchScalarGridSpec(
            num_scalar_prefetch=1,
            grid=(bsz, n_steps + 1),
            in_specs=[seq(n_q), pl.BlockSpec(bias_rows.shape, lambda b, j, pt: (0, 0)), new_keys, new_keys]
                     + [page_spec(p) for p in range(pages_per_step)] * 2 + [seq(n_q)],
            out_specs=seq(n_q),
            scratch_shapes=[pltpu.VMEM((n_heads * n_q, width), BF16), pltpu.VMEM((n_heads * n_q, width), F32),
                            pltpu.VMEM((n_heads * n_q, 1), F32)],
        ),
        out_shape=jax.ShapeDtypeStruct((bsz, n_q, width), BF16),
        compiler_params=pltpu.CompilerParams(dimension_semantics=("arbitrary", "arbitrary"),
                                             vmem_limit_bytes=VMEM_LIMIT),
        name="sb_sample",
    )(page_table, q16, bias_rows, k_new, v_new, *([cache_k] * pages_per_step), *([cache_v] * pages_per_step),
      gate_b)


def _out_proj_kernel(x_ref, ma_ref, mb_ref, w_ref, g_ref, wr_ref, br_ref, x1_ref, h2_ref, lg_ref):
    half = ma_ref.shape[1]
    x1 = x_ref[...] + (_dot(ma_ref[...], w_ref[0:half, :]) + _dot(mb_ref[...], w_ref[half:2 * half, :]))
    x1_ref[...] = x1
    ms = jnp.mean(x1 * x1, axis=-1, keepdims=True)
    h2 = (x1 * lax.rsqrt(ms + RMS_EPS)) * g_ref[...]
    h2_ref[...] = h2.astype(BF16)
    lg_ref[...] = _dot3(_split2(h2), _split2(wr_ref[...])) + br_ref[...]


def _out_proj(x2d, mix_a, mix_b, w_out16, norm_g, w_router_pad, b_router_pad, *, tm):
    m, d = x2d.shape
    tm = min(tm, m)
    row = lambda n: pl.BlockSpec((tm, n), lambda i: (i, 0))
    full = lambda a: pl.BlockSpec(a.shape, lambda i: (0,) * a.ndim)
    n_lg = w_router_pad.shape[1]
    return pl.pallas_call(
        _out_proj_kernel,
        grid=(m // tm,),
        in_specs=[row(d), row(mix_a.shape[1]), row(mix_b.shape[1]), full(w_out16), full(norm_g),
                  full(w_router_pad), full(b_router_pad)],
        out_specs=(row(d), row(d), row(n_lg)),
        out_shape=(jax.ShapeDtypeStruct((m, d), F32), jax.ShapeDtypeStruct((m, d), BF16),
                   jax.ShapeDtypeStruct((m, n_lg), F32)),
        compiler_params=pltpu.CompilerParams(dimension_semantics=("arbitrary",), vmem_limit_bytes=VMEM_LIMIT),
        name="out_proj",
    )(x2d, mix_a, mix_b, w_out16, norm_g, w_router_pad, b_router_pad)


def _moe_kernel(te_ref, nu_ref, x_ref, gate_ref, wgu_ref, bgu_ref, wd_ref, bd_ref, o_ref):
    t = pl.program_id(0)
    d_ff = wd_ref.shape[1]

    @pl.when(t < nu_ref[0])
    def _():
        gu = _dot(x_ref[...], wgu_ref[0]) + bgu_ref[0]
        g_lin = jnp.minimum(gu[:, 0:d_ff], SWIGLU_LIMIT)
        u_lin = jnp.clip(gu[:, d_ff:2 * d_ff], -SWIGLU_LIMIT, SWIGLU_LIMIT)
        act = g_lin * _sigmoid(SWIGLU_ALPHA * g_lin) * (u_lin + 1.0)
        y = _dot(act.astype(BF16), wd_ref[0]) + bd_ref[0]
        o_ref[...] = y * gate_ref[...]

    @pl.when(t >= nu_ref[0])
    def _():
        o_ref[...] = jnp.zeros_like(o_ref)


def _moe(xs16, row_gate, tile_expert, n_used, w_gu16, b_gu, w_d16, b_d, *, tm):
    r, d = xs16.shape
    d_ff = w_d16.shape[1]
    n_tiles = r // tm
    row = lambda n: pl.BlockSpec((tm, n), lambda t, te, nu: (t, 0))
    ex = lambda a: pl.BlockSpec((1,) + a.shape[1:], lambda t, te, nu: (te[t], 0, 0))
    return pl.pallas_call(
        _moe_kernel,
        grid_spec=pltpu.PrefetchScalarGridSpec(
            num_scalar_prefetch=2,
            grid=(n_tiles,),
            in_specs=[row(d), row(1), ex(w_gu16), ex(b_gu), ex(w_d16), ex(b_d)],
            out_specs=row(d),
        ),
        out_shape=jax.ShapeDtypeStruct((r, d), F32),
        compiler_params=pltpu.CompilerParams(dimension_semantics=("arbitrary",), vmem_limit_bytes=VMEM_LIMIT),
        name="moe",
    )(tile_expert, n_used, xs16, row_gate, w_gu16, b_gu, w_d16, b_d)


def _route(logits, n_experts, tm):
    m = logits.shape[0]
    n_assign = m * TOP_K
    top_logit, top_e = lax.top_k(logits, TOP_K)
    gate = jax.nn.softmax(top_logit, axis=-1).reshape(n_assign)
    e_flat = top_e.reshape(n_assign).astype(jnp.int32)
    onehot = (e_flat[:, None] == jnp.arange(n_experts, dtype=jnp.int32)[None, :]).astype(jnp.int32)
    csum = jnp.cumsum(onehot, axis=0)
    rank = jnp.sum((csum - onehot) * onehot, axis=1)
    counts = csum[-1]
    padded = (counts + tm - 1) // tm * tm
    padded_end = jnp.cumsum(padded)
    dest = (padded_end - padded)[e_flat] + rank
    n_tiles = (n_assign + n_experts * (tm - 1)) // tm + 1
    n_rows = n_tiles * tm
    src = jnp.zeros((n_rows,), jnp.int32).at[dest].set(jnp.arange(1, n_assign + 1, dtype=jnp.int32))
    filled = src > 0
    src = jnp.maximum(src - 1, 0)
    row_tok = jnp.where(filled, src // TOP_K, 0)
    row_gate = jnp.where(filled, gate[src], 0.0)
    tile_expert = jnp.minimum(
        jnp.searchsorted(padded_end, jnp.arange(n_tiles, dtype=jnp.int32) * tm, side='right'),
        n_experts - 1).astype(jnp.int32)
    n_used = (padded_end[-1] // tm).astype(jnp.int32).reshape(1)
    return row_tok, row_gate[:, None], tile_expert, n_used, dest


def _pair_block_diag(s):
    bsz, h = s.shape[:2]
    s = s.reshape(bsz, h // 2, 2, HEAD_DIM, HEAD_DIM)
    eye = jnp.eye(2, dtype=s.dtype)
    return jnp.einsum('bpivk,ij->bpivjk', s, eye).reshape(bsz, h // 2, LANES, LANES)


def _pair_diag_blocks(s_bd):
    bsz, n_pairs = s_bd.shape[:2]
    s = s_bd.reshape(bsz, n_pairs, 2, HEAD_DIM, 2, HEAD_DIM)
    return jnp.stack([s[:, :, 0, :, 0, :], s[:, :, 1, :, 1, :]], axis=2).reshape(
        bsz, n_pairs * 2, HEAD_DIM, HEAD_DIM)


def kernel(x_prompt, x_sample, cache_k, cache_v, page_table, state_wkv, state_shift, norm1_g, w_in, mu_shift, w0, w_lora_up, a0, a_lora_up, k_k, k_a, r_k, ln_x_g, ln_x_b, q_norm_g, k_norm_g, sb_bias, w_out, norm2_g, w_router, b_router, w_gate_up, b_gate_up, w_down, b_down):
    d_model = x_prompt.shape[-1]
    rw_proj = mu_shift.shape[0]
    width = w0.shape[0]
    n_heads = width // HEAD_DIM
    n_experts = w_router.shape[1]
    row = lambda a: a.reshape(1, -1).astype(F32)

    w_in16 = w_in.astype(BF16)
    w_out16 = w_out.astype(BF16)
    w_gu16 = w_gate_up.astype(BF16)
    w_d16 = w_down.astype(BF16)
    qg = row(jnp.tile(q_norm_g, n_heads))
    kg = row(jnp.tile(k_norm_g, n_heads))
    lora_up = jnp.zeros((W_RANK + A_RANK, 2 * width), F32)
    lora_up = lora_up.at[:W_RANK, :width].set(w_lora_up).at[W_RANK:, width:].set(a_lora_up)
    w_router_pad = jnp.zeros((d_model, LANES), F32).at[:, :n_experts].set(w_router)
    b_router_pad = jnp.zeros((1, LANES), F32).at[0, :n_experts].set(b_router)
    rwkv_params = (row(mu_shift), row(w0), row(a0), lora_up, row(k_k), row(k_a), row(r_k), row(ln_x_g), row(ln_x_b))

    def front(x, shift0, s0, tile, chunk):
        bsz, t, _ = x.shape
        x2d = x.reshape(bsz * t, d_model)
        p_rw, q16, kb, k16, vb, v16, gate_a, gate_b = _in_proj(
            x2d, row(norm1_g), w_in16, qg, kg, rw_proj=rw_proj, width=width, tm=256)
        seq = lambda a: a.reshape(bsz, t, a.shape[-1])
        p_rw = seq(p_rw)
        mix_a, s_bd = _rwkv(p_rw, seq(gate_a), shift0[:, None, :], _pair_block_diag(s0), *rwkv_params,
                            tile=tile, chunk=chunk)
        return x2d, p_rw, seq(q16), kb, seq(k16), vb, seq(v16), seq(gate_b), mix_a, _pair_diag_blocks(s_bd)

    bp, tp, _ = x_prompt.shape
    bs, ts, _ = x_sample.shape
    xp2d, prw_p, q_p, kb_p, k16_p, vb_p, v16_p, gb_p, mixa_p, wkv_p = front(
        x_prompt, jnp.zeros((bp, rw_proj), F32), jnp.zeros((bp, n_heads, HEAD_DIM, HEAD_DIM), F32), 256, 64)
    xs2d, prw_s, q_s, kb_s, k16_s, vb_s, v16_s, gb_s, mixa_s, wkv_s = front(
        x_sample, state_shift, state_wkv, ts, ts)

    mixb_p = _sb_prompt(q_p, k16_p, v16_p, gb_p, sb_bias.astype(F32), tq=512, tk=128)

    hd = (n_heads, HEAD_DIM)
    n_pool = cache_k.shape[0]
    pages_t = lambda c: c.transpose(0, 2, 3, 1).reshape(n_pool, width, PAGE_SIZE)
    pad_keys = lambda a: jnp.pad(a.reshape(bs, ts, width).transpose(0, 2, 1), ((0, 0), (0, 0), (0, PAGE_SIZE - ts)))
    mixb_s = _sb_sample(q_s, pad_keys(kb_s), pad_keys(vb_s), gb_s, pages_t(cache_k), pages_t(cache_v),
                        page_table.astype(jnp.int32), jnp.repeat(sb_bias.astype(F32), ts)[:, None],
                        pages_per_step=8)

    def back(x2d, mix_a, mix_b):
        m = x2d.shape[0]
        return _out_proj(x2d, mix_a.reshape(m, width), mix_b.reshape(m, width), w_out16, row(norm2_g),
                         w_router_pad, b_router_pad, tm=256)

    x1_p, h2_p, lg_p = back(xp2d, mixa_p, mixb_p)
    x1_s, h2_s, lg_s = back(xs2d, mixa_s, mixb_s)

    x1 = jnp.concatenate([x1_p, x1_s], axis=0)
    h2 = jnp.concatenate([h2_p, h2_s], axis=0)
    logits = jnp.concatenate([lg_p, lg_s], axis=0)[:, :n_experts]
    m_all = x1.shape[0]
    tm_moe = 512
    row_tok, row_gate, tile_expert, n_used, dest = _route(logits, n_experts, tm_moe)
    y_rows = _moe(h2[row_tok], row_gate, tile_expert, n_used, w_gu16, b_gate_up[:, None, :].astype(F32),
                  w_d16, b_down[:, None, :].astype(F32), tm=tm_moe)
    y = x1 + jnp.sum(y_rows[dest.reshape(m_all, TOP_K)], axis=1)

    mp = bp * tp
    return (y[:mp].reshape(bp, tp, d_model), y[mp:].reshape(bs, ts, d_model),
            kb_p.reshape(bp, tp, *hd), vb_p.reshape(bp, tp, *hd),
            kb_s.reshape(bs, ts, *hd), vb_s.reshape(bs, ts, *hd),
            wkv_p, wkv_s, prw_p[:, -1], prw_s[:, -1])
```
